```python
import jax, jax.numpy as jnp
from jax import lax
import numpy as np

D_MODEL = 2048
BATCH = 2
SEQ = 4096
DEPTH = 4

HEAD_DIM = 128
SB_HEADS = 4
FOX_HEADS = 4
MLA_HEADS = 8
SB_W = SB_HEADS * HEAD_DIM
FOX_W = FOX_HEADS * HEAD_DIM
MLA_Q_RANK = 512
MLA_KV_RANK = 256
MLA_NOPE_DIM = 128
MLA_ROPE_DIM = 64
MLA_V_DIM = 128
MLA_QK_DIM = MLA_NOPE_DIM + MLA_ROPE_DIM
MLA_W = MLA_HEADS * MLA_V_DIM
ROPE_THETA = 10000.0
N_BRANCH = 3
BLOCK_Q = 128
RMS_EPS = 1e-6
N_EXPERTS = 32
TOP_K = 4
D_EXPERT = 768
SWIGLU_LIMIT = 7.0
SWIGLU_ALPHA = 1.702
IN_SIZES = (SB_W, SB_W, SB_W,
            FOX_W, FOX_W, FOX_W, FOX_HEADS,
            MLA_Q_RANK, MLA_KV_RANK, MLA_ROPE_DIM,
            N_BRANCH * D_MODEL)
D_IN = sum(IN_SIZES)
N_MOD = 6

kernel_name = "hybrid_sb_fox_mla_moe_adaln"


def rmsnorm(x, g):
    xf = x.astype(jnp.float32)
    y = xf * lax.rsqrt(jnp.mean(xf * xf, axis=-1, keepdims=True) + RMS_EPS)
    return (y * g.astype(jnp.float32)).astype(x.dtype)


def rope_tables(positions):
    half = MLA_ROPE_DIM // 2
    inv_freq = ROPE_THETA ** (-jnp.arange(half, dtype=jnp.float32) / half)
    ang = positions.astype(jnp.float32)[..., None] * inv_freq
    return jnp.cos(ang), jnp.sin(ang)


def apply_rope(x, cos, sin):
    xf = x.astype(jnp.float32)
    x1, x2 = jnp.split(xf, 2, axis=-1)
    return jnp.concatenate([x1 * cos - x2 * sin, x2 * cos + x1 * sin], axis=-1).astype(x.dtype)


def sweep_query_blocks(block_fn, q_like):
    b, s = q_like[0].shape[:2]
    n_blocks = s // BLOCK_Q

    def body(i):
        t0 = i * BLOCK_Q
        sl = [lax.dynamic_slice_in_dim(a, t0, BLOCK_Q, axis=1) for a in q_like]
        return block_fn(t0, *sl)

    out = lax.map(body, jnp.arange(n_blocks, dtype=jnp.int32))
    out = jnp.swapaxes(out, 0, 1)
    return out.reshape((b, s) + out.shape[3:])


def stick_breaking_attention(q, k, v):
    s = q.shape[1]
    scale = q.shape[-1] ** -0.5
    kf = k.astype(jnp.float32)
    vf = v.astype(jnp.float32)
    key_pos = jnp.arange(s, dtype=jnp.int32)

    def block(t0, qb):
        z = jnp.einsum('bqhd,bkhd->bhqk', qb.astype(jnp.float32), kf) * scale
        q_pos = t0 + jnp.arange(BLOCK_Q, dtype=jnp.int32)
        past = key_pos[None, :] < q_pos[:, None]
        log_not = jnp.where(past, jax.nn.log_sigmoid(-z), 0.0)
        later = lax.cumsum(log_not, axis=3, reverse=True) - log_not
        log_a = jnp.where(past, jax.nn.log_sigmoid(z) + later, -jnp.inf)
        out = jnp.einsum('bhqk,bkhd->bqhd', jnp.exp(log_a), vf)
        return out.astype(q.dtype)

    return sweep_query_blocks(block, [q])


def forgetting_attention(q, k, v, log_f):
    s = q.shape[1]
    scale = q.shape[-1] ** -0.5
    kf = k.astype(jnp.float32)
    vf = v.astype(jnp.float32)
    cum = jnp.cumsum(log_f, axis=1)
    cum_k = jnp.transpose(cum, (0, 2, 1))[:, :, None, :]
    key_pos = jnp.arange(s, dtype=jnp.int32)

    def block(t0, qb, cum_q):
        logits = jnp.einsum('bqhd,bkhd->bhqk', qb.astype(jnp.float32), kf) * scale
        logits = logits + jnp.transpose(cum_q, (0, 2, 1))[:, :, :, None] - cum_k
        q_pos = t0 + jnp.arange(BLOCK_Q, dtype=jnp.int32)
        causal = key_pos[None, :] <= q_pos[:, None]
        p = jax.nn.softmax(jnp.where(causal, logits, -jnp.inf), axis=-1)
        return jnp.einsum('bhqk,bkhd->bqhd', p, vf).astype(q.dtype)

    return sweep_query_blocks(block, [q, cum])


def mla_attention(q_nope, q_rope, k_nope, k_rope, v):
    s = q_nope.shape[1]
    scale = MLA_QK_DIM ** -0.5
    knf = k_nope.astype(jnp.float32)
    krf = k_rope.astype(jnp.float32)
    vf = v.astype(jnp.float32)
    key_pos = jnp.arange(s, dtype=jnp.int32)

    def block(t0, qn, qr):
        logits = (jnp.einsum('bqhd,bkhd->bhqk', qn.astype(jnp.float32), knf)
                  + jnp.einsum('bqhd,bkd->bhqk', qr.astype(jnp.float32), krf)) * scale
        q_pos = t0 + jnp.arange(BLOCK_Q, dtype=jnp.int32)
        causal = key_pos[None, :] <= q_pos[:, None]
        p = jax.nn.softmax(jnp.where(causal, logits, -jnp.inf), axis=-1)
        return jnp.einsum('bhqk,bkhd->bqhd', p, vf).astype(q_nope.dtype)

    return sweep_query_blocks(block, [q_nope, q_rope])


def hybrid_mixer(h, cos, sin, w_in, b_forget, g_q_norm, g_kv_norm, w_q_up, w_kv_up,
                 w_sb_proj, w_fox_proj, w_mla_proj, w_out):
    b, s, _ = h.shape
    proj = h @ w_in
    split_at = [int(i) for i in np.cumsum(IN_SIZES)[:-1]]
    (sb_q, sb_k, sb_v, fx_q, fx_k, fx_v, fx_f, c_q, c_kv, k_rope, gate_logits) = jnp.split(proj, split_at, axis=-1)

    y_sb = stick_breaking_attention(sb_q.reshape(b, s, SB_HEADS, HEAD_DIM),
                                    sb_k.reshape(b, s, SB_HEADS, HEAD_DIM),
                                    sb_v.reshape(b, s, SB_HEADS, HEAD_DIM)).reshape(b, s, SB_W)

    log_f = jax.nn.log_sigmoid(fx_f.astype(jnp.float32) + b_forget.astype(jnp.float32))
    y_fox = forgetting_attention(fx_q.reshape(b, s, FOX_HEADS, HEAD_DIM),
                                 fx_k.reshape(b, s, FOX_HEADS, HEAD_DIM),
                                 fx_v.reshape(b, s, FOX_HEADS, HEAD_DIM), log_f).reshape(b, s, FOX_W)

    q = (rmsnorm(c_q, g_q_norm) @ w_q_up).reshape(b, s, MLA_HEADS, MLA_QK_DIM)
    q_nope, q_rope = jnp.split(q, [MLA_NOPE_DIM], axis=-1)
    q_rope = apply_rope(q_rope, cos[:, :, None, :], sin[:, :, None, :])
    kv = (rmsnorm(c_kv, g_kv_norm) @ w_kv_up).reshape(b, s, MLA_HEADS, MLA_NOPE_DIM + MLA_V_DIM)
    k_nope, v = jnp.split(kv, [MLA_NOPE_DIM], axis=-1)
    k_rope = apply_rope(k_rope, cos, sin)
    y_mla = mla_attention(q_nope, q_rope, k_nope, k_rope, v).reshape(b, s, MLA_W)

    g_sb, g_fox, g_mla = jnp.split(jax.nn.sigmoid(gate_logits), N_BRANCH, axis=-1)
    merged = g_sb * (y_sb @ w_sb_proj) + g_fox * (y_fox @ w_fox_proj) + g_mla * (y_mla @ w_mla_proj)
    return merged @ w_out


def moe_ffn(h, w_router, b_router, w_gate_up, b_gate_up, w_down, b_down):
    b, s, d = h.shape
    t = h.reshape(b * s, d)
    logits = (t @ w_router).astype(jnp.float32) + b_router.astype(jnp.float32)
    top_logits, top_idx = lax.top_k(logits, TOP_K)
    top_w = jax.nn.softmax(top_logits, axis=-1)
    flat_e = top_idx.reshape(-1)
    order = jnp.argsort(flat_e)
    sorted_e = flat_e[order]
    token_of = order // TOP_K
    xs = t[token_of]
    group_sizes = jnp.bincount(flat_e, length=N_EXPERTS).astype(jnp.int32)
    gu = lax.ragged_dot(xs, w_gate_up, group_sizes) + b_gate_up[sorted_e]
    gate, up = jnp.split(gu, 2, axis=-1)
    gate = jnp.minimum(gate, SWIGLU_LIMIT)
    up = jnp.clip(up, -SWIGLU_LIMIT, SWIGLU_LIMIT)
    act = (up + 1.0) * (gate * jax.nn.sigmoid(SWIGLU_ALPHA * gate))
    out = lax.ragged_dot(act, w_down, group_sizes) + b_down[sorted_e]
    out = out * top_w.reshape(-1)[order][:, None].astype(out.dtype)
    y = jnp.zeros_like(t).at[token_of].add(out)
    return y.reshape(b, s, d)


def setup_inputs(seed: int = 0) -> dict:
    key = jax.random.key(seed)
    ks = jax.random.split(key, 32)
    f32 = jnp.float32
    L, D = DEPTH, D_MODEL

    def nrm(k, shape, fan_in, mult=1.0):
        return jax.random.normal(k, shape, f32) * (mult * fan_in ** -0.5)

    def gain(k, shape):
        return 1.0 + 0.02 * jax.random.normal(k, shape, f32)

    def small(k, shape):
        return 0.01 * jax.random.normal(k, shape, f32)

    offset = jax.random.randint(ks[2], (BATCH, 1), 0, 1024, dtype=jnp.int32)
    positions = offset + jnp.arange(SEQ, dtype=jnp.int32)[None, :]
    return {
        'x': jax.random.normal(ks[0], (BATCH, SEQ, D), f32),
        'c': jax.random.normal(ks[1], (BATCH, D), f32),
        'positions': positions,
        'w_ada': nrm(ks[3], (L, D, N_MOD * D), D, 0.5),
        'b_ada': small(ks[4], (L, N_MOD * D)),
        'g_mix_norm': gain(ks[5], (L, D)),
        'g_ffn_norm': gain(ks[6], (L, D)),
        'w_in': nrm(ks[7], (L, D, D_IN), D),
        'b_forget': jax.random.uniform(ks[8], (L, FOX_HEADS), f32, 1.0, 4.0),
        'g_q_norm': gain(ks[9], (L, MLA_Q_RANK)),
        'g_kv_norm': gain(ks[10], (L, MLA_KV_RANK)),
        'w_q_up': nrm(ks[11], (L, MLA_Q_RANK, MLA_HEADS * MLA_QK_DIM), MLA_Q_RANK),
        'w_kv_up': nrm(ks[12], (L, MLA_KV_RANK, MLA_HEADS * (MLA_NOPE_DIM + MLA_V_DIM)), MLA_KV_RANK),
        'w_sb_proj': nrm(ks[13], (L, SB_W, D), SB_W),
        'w_fox_proj': nrm(ks[14], (L, FOX_W, D), FOX_W),
        'w_mla_proj': nrm(ks[15], (L, MLA_W, D), MLA_W),
        'w_out': nrm(ks[16], (L, D, D), D),
        'w_router': nrm(ks[17], (L, D, N_EXPERTS), D),
        'b_router': small(ks[18], (L, N_EXPERTS)),
        'w_gate_up': nrm(ks[19], (L, N_EXPERTS, D, 2 * D_EXPERT), D),
        'b_gate_up': small(ks[20], (L, N_EXPERTS, 2 * D_EXPERT)),
        'w_down': nrm(ks[21], (L, N_EXPERTS, D_EXPERT, D), D_EXPERT),
        'b_down': small(ks[22], (L, N_EXPERTS, D)),
        'g_final': gain(ks[23], (D,)),
    }


def reference(x, c, positions, w_ada, b_ada, g_mix_norm, g_ffn_norm, w_in, b_forget,
              g_q_norm, g_kv_norm, w_q_up, w_kv_up, w_sb_proj, w_fox_proj, w_mla_proj, w_out,
              w_router, b_router, w_gate_up, b_gate_up, w_down, b_down, g_final):
    b, s, d = x.shape
    cos, sin = rope_tables(positions)
    cond = jax.nn.silu(c)
    for layer in range(DEPTH):
        mod = (cond @ w_ada[layer] + b_ada[layer]).reshape(b, N_MOD, d)
        shift_m, scale_m, gate_m, shift_f, scale_f, gate_f = [mod[:, i][:, None, :] for i in range(N_MOD)]
        h = rmsnorm(x, g_mix_norm[layer]) * (1.0 + scale_m) + shift_m
        x = x + gate_m * hybrid_mixer(h, cos, sin, w_in[layer], b_forget[layer], g_q_norm[layer],
                                      g_kv_norm[layer], w_q_up[layer], w_kv_up[layer], w_sb_proj[layer],
                                      w_fox_proj[layer], w_mla_proj[layer], w_out[layer])
        h = rmsnorm(x, g_ffn_norm[layer]) * (1.0 + scale_f) + shift_f
        x = x + gate_f * moe_ffn(h, w_router[layer], b_router[layer], w_gate_up[layer],
                                 b_gate_up[layer], w_down[layer], b_down[layer])
    return rmsnorm(x, g_final)
```

```python
import functools

import numpy as np
import jax
import jax.numpy as jnp
from jax import lax
from jax.experimental import pallas as pl
from jax.experimental.pallas import tpu as pltpu

F32 = jnp.float32
BF16 = jnp.bfloat16
I32 = jnp.int32
U32 = jnp.uint32

HEAD_DIM = 128
SB_HEADS = 4
FOX_HEADS = 4
MLA_HEADS = 8
SB_W = SB_HEADS * HEAD_DIM
FOX_W = FOX_HEADS * HEAD_DIM
MLA_Q_RANK = 512
MLA_KV_RANK = 256
MLA_NOPE_DIM = 128
MLA_ROPE_DIM = 64
MLA_V_DIM = 128
MLA_QK_DIM = MLA_NOPE_DIM + MLA_ROPE_DIM
MLA_W = MLA_HEADS * MLA_V_DIM
ROPE_THETA = 10000.0
N_BRANCH = 3
RMS_EPS = 1e-6
N_EXPERTS = 32
TOP_K = 4
SWIGLU_LIMIT = 7.0
SWIGLU_ALPHA = 1.702
N_MOD = 6

LANES = 128
SUBLANES = 8
MIB = 1024 * 1024
NEG_BIG = -1e30

NORM_ROWS = 256

PROJ_TN = 512
T_SBQ, T_SBK, T_SBV, T_FXQ, T_FXK, T_FXV, T_CQ, T_AUX, T_GATE = 0, 1, 2, 3, 4, 5, 6, 7, 8
AUX_KR = MLA_KV_RANK
AUX_FF = MLA_KV_RANK + LANES


def _cparams(sem, vmem_mib):
    return pltpu.CompilerParams(dimension_semantics=sem, vmem_limit_bytes=vmem_mib * MIB)


def _sigmoid(x):
    return 1.0 / (1.0 + jnp.exp(-x))


def _split2(x):
    hi = x.astype(BF16)
    lo = (x - hi.astype(F32)).astype(BF16)
    return hi, lo


def _split3(x):
    p1 = x.astype(BF16)
    r1 = x - p1.astype(F32)
    p2 = r1.astype(BF16)
    r2 = r1 - p2.astype(F32)
    return p1, p2, r2.astype(BF16)


def _dot(a, b):
    return jnp.dot(a, b, preferred_element_type=F32)


def _dot_nt(a, b):
    return lax.dot_general(a, b, (((1,), (1,)), ((), ())), preferred_element_type=F32)


def _rope(x, c, s):
    return x * c + pltpu.roll(x, LANES // 2, 1) * s


def _pack_bf16_pair(a, b):
    ua = lax.bitcast_convert_type(a.astype(BF16).astype(F32), U32)
    ub = lax.bitcast_convert_type(b.astype(BF16).astype(F32), U32)
    return (ua >> 16) | (ub & jnp.uint32(0xFFFF0000))


def _unpack_bf16_pair(w):
    a = lax.bitcast_convert_type(w << 16, F32)
    b = lax.bitcast_convert_type(w & jnp.uint32(0xFFFF0000), F32)
    return a, b


def _adaln_kernel(c_ref, w_ref, b_ref, o_ref):
    c = c_ref[...]
    cond = c * _sigmoid(c)
    c1, c2 = _split2(cond)
    w1, w2 = _split2(w_ref[0])
    o_ref[0] = _dot(c1, w1) + _dot(c1, w2) + _dot(c2, w1) + b_ref[0]


def _adaln(c, w_ada, b_ada):
    n_layer, d, n6 = w_ada.shape
    b = c.shape[0]
    tn = 1024
    c8 = jnp.zeros((SUBLANES, d), F32).at[:b].set(c)
    out = pl.pallas_call(
        _adaln_kernel,
        out_shape=jax.ShapeDtypeStruct((n_layer, SUBLANES, n6), F32),
        grid=(n_layer, n6 // tn),
        in_specs=[
            pl.BlockSpec((SUBLANES, d), lambda l, n: (0, 0)),
            pl.BlockSpec((1, d, tn), lambda l, n: (l, 0, n)),
            pl.BlockSpec((1, 1, tn), lambda l, n: (l, 0, n)),
        ],
        out_specs=pl.BlockSpec((1, SUBLANES, tn), lambda l, n: (l, 0, n)),
        compiler_params=_cparams(("parallel", "parallel"), 48),
        name="adaln",
    )(c8, w_ada, b_ada.reshape(n_layer, 1, n6))
    return out[:, :b].reshape(n_layer, b, N_MOD, d)


def _rope_kernel(pos_ref, invf_ref, cm_ref, sg_ref, c_ref, s_ref):
    ang = pos_ref[...].astype(F32) * invf_ref[...]
    c_ref[...] = jnp.cos(ang) * cm_ref[...]
    s_ref[...] = jnp.sin(ang) * sg_ref[...]


def _rope_tables(positions):
    t = positions.size
    half = MLA_ROPE_DIM // 2
    inv_freq = ROPE_THETA ** (-jnp.arange(half, dtype=F32) / half)
    z = jnp.zeros((half,), F32)
    o = jnp.ones((half,), F32)
    invf = jnp.concatenate([inv_freq, z, inv_freq, z]).reshape(1, LANES)
    cmask = jnp.concatenate([o, z, o, z]).reshape(1, LANES)
    sgn = jnp.concatenate([-o, z, o, z]).reshape(1, LANES)
    tm = min(t, 1024)
    row = pl.BlockSpec((1, LANES), lambda i: (0, 0))
    return pl.pallas_call(
        _rope_kernel,
        out_shape=(jax.ShapeDtypeStruct((t, LANES), F32),) * 2,
        grid=(t // tm,),
        in_specs=[pl.BlockSpec((tm, 1), lambda i: (i, 0)), row, row, row],
        out_specs=(pl.BlockSpec((tm, LANES), lambda i: (i, 0)),) * 2,
        compiler_params=_cparams(("parallel",), 32),
        name="rope_tables",
    )(positions.reshape(t, 1), invf, cmask, sgn)


def _premix_kernel(x_ref, g_ref, sc_ref, sh_ref, w_ref, o_ref, aux_ref, h_scr):
    n = pl.program_id(1)

    @pl.when(n == 0)
    def _():
        def norm_rows(r, carry):
            rows = pl.ds(pl.multiple_of(r * NORM_ROWS, NORM_ROWS), NORM_ROWS)
            x = x_ref[rows, :]
            y = x * lax.rsqrt(jnp.mean(x * x, axis=-1, keepdims=True) + RMS_EPS) * g_ref[...]
            h_scr[rows, :] = (y * (1.0 + sc_ref[0]) + sh_ref[0]).astype(BF16)
            return carry

        lax.fori_loop(0, x_ref.shape[0] // NORM_ROWS, norm_rows, 0)

    acc = _dot(h_scr[...], w_ref[...])
    o_ref[...] = acc.astype(BF16)

    @pl.when(n == T_AUX)
    def _():
        aux_ref[...] = acc[:, AUX_FF:AUX_FF + LANES]


def _premix(x2, g, scale, shift, w1, seq):
    t, d = x2.shape
    n1 = w1.shape[1]
    tm = min(seq, 1024)
    per_b = seq // tm
    bvec = pl.BlockSpec((1, 1, d), lambda i, n: (i // per_b, 0, 0))
    return pl.pallas_call(
        _premix_kernel,
        out_shape=(jax.ShapeDtypeStruct((t, n1), BF16), jax.ShapeDtypeStruct((t, LANES), F32)),
        grid=(t // tm, n1 // PROJ_TN),
        in_specs=[
            pl.BlockSpec((tm, d), lambda i, n: (i, 0)),
            pl.BlockSpec((1, d), lambda i, n: (0, 0)),
            bvec, bvec,
            pl.BlockSpec((d, PROJ_TN), lambda i, n: (0, n)),
        ],
        out_specs=(pl.BlockSpec((tm, PROJ_TN), lambda i, n: (i, n)),
                   pl.BlockSpec((tm, LANES), lambda i, n: (i, 0))),
        scratch_shapes=[pltpu.VMEM((tm, d), BF16)],
        compiler_params=_cparams(("parallel", "arbitrary"), 48),
        name="premix_proj",
    )(x2, g.reshape(1, d), scale, shift, w1)


def _mla_prep_kernel(cq_ref, t7_ref, c_ref, s_ref, gq_ref, gkv_ref, wq_ref, wkv_ref,
                     qn_ref, qr_ref, kn_ref, v_ref, kr_ref):
    cq = cq_ref[...].astype(F32)
    cqn = cq * lax.rsqrt(jnp.mean(cq * cq, axis=-1, keepdims=True) + RMS_EPS) * gq_ref[...]
    q = _dot(cqn.astype(BF16), wq_ref[...]) * (MLA_QK_DIM ** -0.5)
    nw = MLA_HEADS * MLA_NOPE_DIM
    qn_ref[...] = q[:, :nw].astype(BF16)
    c = c_ref[...]
    s = s_ref[...]
    for h in range(MLA_HEADS):
        xr = q[:, nw + h * LANES: nw + (h + 1) * LANES]
        qr_ref[:, h * LANES:(h + 1) * LANES] = _rope(xr, c, s).astype(BF16)
    t7 = t7_ref[...].astype(F32)
    ckv = t7[:, :MLA_KV_RANK]
    ckvn = ckv * lax.rsqrt(jnp.mean(ckv * ckv, axis=-1, keepdims=True) + RMS_EPS) * gkv_ref[...]
    kv = _dot(ckvn.astype(BF16), wkv_ref[...])
    kn_ref[...] = kv[:, :nw].astype(BF16)
    v_ref[...] = kv[:, nw:].astype(BF16)
    kr_ref[...] = _rope(t7[:, AUX_KR:AUX_KR + LANES], c, s).astype(BF16)


def _mla_prep(proj, cos_t, sin_t, gq, gkv, wq, wkv):
    t = proj.shape[0]
    tm = min(t, 512)
    nw = MLA_HEADS * MLA_NOPE_DIM
    full = lambda shape: pl.BlockSpec(shape, lambda i: (0, 0))
    rows = lambda w: pl.BlockSpec((tm, w), lambda i: (i, 0))
    return pl.pallas_call(
        _mla_prep_kernel,
        out_shape=(jax.ShapeDtypeStruct((t, nw), BF16),) * 4 + (jax.ShapeDtypeStruct((t, LANES), BF16),),
        grid=(t // tm,),
        in_specs=[
            pl.BlockSpec((tm, PROJ_TN), lambda i: (i, T_CQ)),
            pl.BlockSpec((tm, PROJ_TN), lambda i: (i, T_AUX)),
            rows(LANES), rows(LANES),
            full((1, MLA_Q_RANK)), full((1, MLA_KV_RANK)),
            full(wq.shape), full(wkv.shape),
        ],
        out_specs=(rows(nw),) * 4 + (rows(LANES),),
        compiler_params=_cparams(("parallel",), 48),
        name="mla_prep",
    )(proj, proj, cos_t, sin_t, gq.reshape(1, -1), gkv.reshape(1, -1), wq, wkv)


def _fox_gate_kernel(aux_ref, b_ref, frep_ref, ft_ref, carry):
    j = pl.program_id(1)
    tm = aux_ref.shape[0]

    @pl.when(j == 0)
    def _():
        carry[...] = jnp.zeros_like(carry)

    xx = aux_ref[...] + b_ref[...]
    lf = jnp.minimum(xx, 0.0) - jnp.log1p(jnp.exp(-jnp.abs(xx)))
    r = lax.broadcasted_iota(I32, (tm, tm), 0)
    cc = lax.broadcasted_iota(I32, (tm, tm), 1)
    tri = (cc <= r).astype(BF16)
    p1, p2, p3 = _split3(lf)
    f = _dot(tri, p1) + _dot(tri, p2) + _dot(tri, p3) + carry[...]
    carry[...] = f[tm - 1:tm, :]
    ft = f.T
    for h in range(FOX_HEADS):
        frep_ref[h] = jnp.broadcast_to(f[:, h:h + 1], (tm, LANES))
        ft_ref[h, 0] = ft[h:h + 1, :]


def _fox_gates(aux, b_forget, batch, seq):
    tm = min(seq, 256)
    ns = seq // tm
    bpad = jnp.zeros((1, LANES), F32).at[0, :FOX_HEADS].set(b_forget)
    return pl.pallas_call(
        _fox_gate_kernel,
        out_shape=(jax.ShapeDtypeStruct((FOX_HEADS, batch * seq, LANES), F32),
                   jax.ShapeDtypeStruct((FOX_HEADS, batch, 1, seq), F32)),
        grid=(batch, ns),
        in_specs=[pl.BlockSpec((tm, LANES), lambda b, j: (b * ns + j, 0)),
                  pl.BlockSpec((1, LANES), lambda b, j: (0, 0))],
        out_specs=(pl.BlockSpec((FOX_HEADS, tm, LANES), lambda b, j: (0, b * ns + j, 0)),
                   pl.BlockSpec((FOX_HEADS, 1, 1, tm), lambda b, j: (0, b, 0, j))),
        scratch_shapes=[pltpu.VMEM((1, LANES), F32)],
        compiler_params=_cparams(("parallel", "arbitrary"), 32),
        name="fox_gates",
    )(aux, bpad)


def _softmax_block(s, v, m_scr, l_scr, acc_scr):
    m_prev = m_scr[...]
    m_new = jnp.maximum(m_prev, jnp.max(s, axis=1, keepdims=True))
    alpha = jnp.exp(m_prev - m_new)
    reps = s.shape[1] // LANES
    p = jnp.exp(s - jnp.concatenate([m_new] * reps, axis=1))
    l_scr[...] = alpha * l_scr[...] + jnp.sum(p, axis=1, keepdims=True)
    acc_scr[...] = alpha * acc_scr[...] + _dot(p.astype(BF16), v)
    m_scr[...] = m_new


def _causal_softmax_attention(i, tq, q, load_k, load_v, bias, o_ref, m_scr, l_scr, acc_scr):
    m_scr[...] = jnp.full_like(m_scr, NEG_BIG)
    l_scr[...] = jnp.zeros_like(l_scr)
    acc_scr[...] = jnp.zeros_like(acc_scr)

    row = lax.broadcasted_iota(I32, (tq, tq), 0)
    col = lax.broadcasted_iota(I32, (tq, tq), 1)
    s = bias(_dot_nt(q, load_k(i)), i)
    _softmax_block(jnp.where(col <= row, s, NEG_BIG), load_v(i), m_scr, l_scr, acc_scr)

    def body(j, carry):
        _softmax_block(bias(_dot_nt(q, load_k(j)), j), load_v(j), m_scr, l_scr, acc_scr)
        return carry

    lax.fori_loop(0, i, body, 0)
    o_ref[...] = (acc_scr[...] / l_scr[...]).astype(o_ref.dtype)


def _fox_attn_kernel(q_ref, k_ref, v_ref, fq_ref, fk_ref, o_ref, m_scr, l_scr, acc_scr):
    i = pl.program_id(2)
    tq = q_ref.shape[0]
    q = (q_ref[...].astype(F32) * (HEAD_DIM ** -0.5)).astype(BF16)
    fq = fq_ref[0]
    reps = tq // LANES

    def blk(j):
        return pl.ds(pl.multiple_of(j * tq, tq), tq)

    def bias(s, j):
        return s + jnp.concatenate([fq] * reps, axis=1) - fk_ref[0, 0, :, blk(j)]

    _causal_softmax_attention(i, tq, q, lambda j: k_ref[blk(j), :], lambda j: v_ref[blk(j), :],
                              bias, o_ref, m_scr, l_scr, acc_scr)


def _attn_tq(seq):
    return min(seq, 256)


def _attn_scratch(tq):
    return [pltpu.VMEM((tq, LANES), F32)] * 3


def _fox_attention(proj, frep, ft, batch, seq):
    tq = _attn_tq(seq)
    nq = seq // tq
    t = batch * seq
    qspec = lambda base: pl.BlockSpec((tq, HEAD_DIM), lambda b, h, i: (b * nq + i, base + h))
    kvspec = lambda base: pl.BlockSpec((seq, HEAD_DIM), lambda b, h, i: (b, base + h))
    per_tile = PROJ_TN // HEAD_DIM
    return pl.pallas_call(
        _fox_attn_kernel,
        out_shape=jax.ShapeDtypeStruct((t, FOX_W), BF16),
        grid=(batch, FOX_HEADS, nq),
        in_specs=[qspec(T_FXQ * per_tile), kvspec(T_FXK * per_tile), kvspec(T_FXV * per_tile),
                  pl.BlockSpec((1, tq, LANES), lambda b, h, i: (h, b * nq + i, 0)),
                  pl.BlockSpec((1, 1, 1, seq), lambda b, h, i: (h, b, 0, 0))],
        out_specs=pl.BlockSpec((tq, HEAD_DIM), lambda b, h, i: (b * nq + i, h)),
        scratch_shapes=_attn_scratch(tq),
        compiler_params=_cparams(("parallel", "parallel", "arbitrary"), 48),
        name="fox_attention",
    )(proj, proj, proj, frep, ft)


def _mla_attn_kernel(qn_ref, qr_ref, kn_ref, kr_ref, v_ref, o_ref, m_scr, l_scr, acc_scr):
    i = pl.program_id(2)
    tq = qn_ref.shape[0]
    q = jnp.concatenate([qn_ref[...], qr_ref[...]], axis=1)

    def blk(j):
        return pl.ds(pl.multiple_of(j * tq, tq), tq)

    def load_k(j):
        return jnp.concatenate([kn_ref[blk(j), :], kr_ref[blk(j), :]], axis=1)

    _causal_softmax_attention(i, tq, q, load_k, lambda j: v_ref[blk(j), :],
                              lambda s, j: s, o_ref, m_scr, l_scr, acc_scr)


def _mla_attention(qn, qr, kn, kr, v, batch, seq):
    tq = _attn_tq(seq)
    nq = seq // tq
    t = batch * seq
    qspec = pl.BlockSpec((tq, HEAD_DIM), lambda b, h, i: (b * nq + i, h))
    kvspec = pl.BlockSpec((seq, HEAD_DIM), lambda b, h, i: (b, h))
    return pl.pallas_call(
        _mla_attn_kernel,
        out_shape=jax.ShapeDtypeStruct((t, MLA_W), BF16),
        grid=(batch, MLA_HEADS, nq),
        in_specs=[qspec, qspec, kvspec,
                  pl.BlockSpec((seq, LANES), lambda b, h, i: (b, 0)),
                  kvspec],
        out_specs=qspec,
        scratch_shapes=_attn_scratch(tq),
        compiler_params=_cparams(("parallel", "parallel", "arbitrary"), 48),
        name="mla_attention",
    )(qn, qr, kn, kr, v)


SB_TK = 128


def _sb_attn_kernel(q_ref, k_ref, v_ref, o_ref, run_scr, acc_scr):
    i = pl.program_id(2)
    tq = q_ref.shape[0]
    tk = SB_TK
    per_q = tq // tk
    q = (q_ref[...].astype(F32) * (HEAD_DIM ** -0.5)).astype(BF16)
    run_scr[...] = jnp.zeros_like(run_scr)
    acc_scr[...] = jnp.zeros_like(acc_scr)

    rj = lax.broadcasted_iota(I32, (tk, 2 * tk), 0)
    cs = lax.broadcasted_iota(I32, (tk, 2 * tk), 1)
    later_or_all = ((cs >= tk) | (rj > cs)).astype(BF16)

    def block(jb, past):
        ks = pl.ds(pl.multiple_of(jb * tk, tk), tk)
        z = _dot_nt(q, k_ref[ks, :])
        log_not = -(jnp.maximum(z, 0.0) + jnp.log1p(jnp.exp(-jnp.abs(z))))
        if past is not None:
            log_not = jnp.where(past, log_not, 0.0)
        hi, lo = _split2(log_not)
        sums = _dot(hi, later_or_all) + _dot(lo, later_or_all)
        a = jnp.exp(z + log_not + sums[:, :tk] + run_scr[...])
        if past is not None:
            a = jnp.where(past, a, 0.0)
        acc_scr[...] += _dot(a.astype(BF16), v_ref[ks, :])
        run_scr[...] += sums[:, tk:]

    qpos = lax.broadcasted_iota(I32, (tq, tk), 0)
    kpos = lax.broadcasted_iota(I32, (tq, tk), 1)
    for d in reversed(range(per_q)):
        block(i * per_q + d, (kpos + d * tk) < qpos)

    def body(it, carry):
        block(i * per_q - 1 - it, None)
        return carry

    lax.fori_loop(0, i * per_q, body, 0)
    o_ref[...] = acc_scr[...].astype(o_ref.dtype)


def _sb_attention(proj, batch, seq):
    tq = _attn_tq(seq)
    nq = seq // tq
    t = batch * seq
    per_tile = PROJ_TN // HEAD_DIM
    qspec = lambda base: pl.BlockSpec((tq, HEAD_DIM), lambda b, h, i: (b * nq + i, base + h))
    kvspec = lambda base: pl.BlockSpec((seq, HEAD_DIM), lambda b, h, i: (b, base + h))
    return pl.pallas_call(
        _sb_attn_kernel,
        out_shape=jax.ShapeDtypeStruct((t, SB_W), BF16),
        grid=(batch, SB_HEADS, nq),
        in_specs=[qspec(T_SBQ * per_tile), kvspec(T_SBK * per_tile), kvspec(T_SBV * per_tile)],
        out_specs=pl.BlockSpec((tq, HEAD_DIM), lambda b, h, i: (b * nq + i, h)),
        scratch_shapes=[pltpu.VMEM((tq, SB_TK), F32), pltpu.VMEM((tq, HEAD_DIM), F32)],
        compiler_params=_cparams(("parallel", "parallel", "arbitrary"), 48),
        name="sb_attention",
    )(proj, proj, proj)


def _merge_kernel(ysb_ref, yfx_ref, yml_ref, wsb_ref, wfx_ref, wml_ref, gsb_ref, gfx_ref, gml_ref, o_ref):
    def term(g_ref, y_ref, w_ref):
        return _sigmoid(g_ref[...].astype(F32)) * _dot(y_ref[...], w_ref[...])

    o_ref[...] = (term(gsb_ref, ysb_ref, wsb_ref) + term(gfx_ref, yfx_ref, wfx_ref)
                  + term(gml_ref, yml_ref, wml_ref)).astype(BF16)


def _merge(y_sb, y_fox, y_mla, w_sb, w_fox, w_mla, proj, d):
    t = y_sb.shape[0]
    tm = min(t, 1024)
    tn = PROJ_TN
    nd = d // tn
    ys = lambda w: pl.BlockSpec((tm, w), lambda i, n: (i, 0))
    ws = lambda k: pl.BlockSpec((k, tn), lambda i, n: (0, n))
    gs = lambda br: pl.BlockSpec((tm, tn), lambda i, n: (i, T_GATE + br * nd + n))
    return pl.pallas_call(
        _merge_kernel,
        out_shape=jax.ShapeDtypeStruct((t, d), BF16),
        grid=(t // tm, nd),
        in_specs=[ys(SB_W), ys(FOX_W), ys(MLA_W), ws(SB_W), ws(FOX_W), ws(MLA_W), gs(0), gs(1), gs(2)],
        out_specs=pl.BlockSpec((tm, tn), lambda i, n: (i, n)),
        compiler_params=_cparams(("parallel", "parallel"), 48),
        name="merge",
    )(y_sb, y_fox, y_mla, w_sb, w_fox, w_mla, proj, proj, proj)


def _outproj_kernel(m_ref, w_ref, x_ref, g_ref, o_ref):
    o_ref[...] = x_ref[...] + g_ref[0] * _dot(m_ref[...], w_ref[...])


def _outproj(merged, w_out, x2, gate, seq):
    t, d = x2.shape
    tm = min(seq, 1024)
    per_b = seq // tm
    tn = PROJ_TN
    return pl.pallas_call(
        _outproj_kernel,
        out_shape=jax.ShapeDtypeStruct((t, d), F32),
        grid=(t // tm, d // tn),
        in_specs=[pl.BlockSpec((tm, d), lambda i, n: (i, 0)),
                  pl.BlockSpec((d, tn), lambda i, n: (0, n)),
                  pl.BlockSpec((tm, tn), lambda i, n: (i, n)),
                  pl.BlockSpec((1, 1, tn), lambda i, n: (i // per_b, 0, n))],
        out_specs=pl.BlockSpec((tm, tn), lambda i, n: (i, n)),
        compiler_params=_cparams(("parallel", "parallel"), 48),
        name="outproj",
    )(merged, w_out, x2, gate)


ROW_TILE = SUBLANES


def _store_packed_rows(ref, val):
    tm, d = val.shape
    half = d // 2
    packed = _pack_bf16_pair(val[:, :half], val[:, half:])
    for c in range(ROW_TILE):
        ref[pl.ds(c, tm, stride=ROW_TILE), :] = packed[:, c * LANES:(c + 1) * LANES]


def _load_packed_rows(ref, base, tm):
    los, his = [], []
    for c in range(ROW_TILE):
        lo, hi = _unpack_bf16_pair(ref[pl.ds(base * ROW_TILE + c, tm, stride=ROW_TILE), :])
        los.append(lo)
        his.append(hi)
    return jnp.concatenate(los + his, axis=1)


def _router_kernel(x_ref, g_ref, sc_ref, sh_ref, wr_ref, br_ref,
                   h_ref, idx_ref, w_ref, rank_ref, cnt_ref, carry):
    step = pl.program_id(0)
    tm = x_ref.shape[0]

    @pl.when(step == 0)
    def _():
        carry[...] = jnp.zeros_like(carry)

    x = x_ref[...]
    y = x * lax.rsqrt(jnp.mean(x * x, axis=-1, keepdims=True) + RMS_EPS) * g_ref[...]
    h = y * (1.0 + sc_ref[0]) + sh_ref[0]
    _store_packed_rows(h_ref, h)

    h1, h2 = _split2(h)
    w1, w2 = _split2(wr_ref[...])
    logits = _dot(h1, w1) + _dot(h1, w2) + _dot(h2, w1) + br_ref[...]

    lane = lax.broadcasted_iota(I32, (tm, LANES), 1).astype(F32)
    lg = jnp.where(lane < N_EXPERTS, logits, NEG_BIG)
    vals, idxs = [], []
    for _ in range(TOP_K):
        m = jnp.max(lg, axis=1, keepdims=True)
        ix = jnp.min(jnp.where(lg == m, lane, float(LANES)), axis=1, keepdims=True)
        vals.append(m)
        idxs.append(ix)
        lg = jnp.where(lane == ix, NEG_BIG, lg)

    es = [jnp.exp(v - vals[0]) for v in vals]
    denom = es[0] + es[1] + es[2] + es[3]

    onehots = [(lane == ix) for ix in idxs]
    chosen = sum(oh.astype(F32) for oh in onehots)
    r = lax.broadcasted_iota(I32, (tm, tm), 0)
    cc = lax.broadcasted_iota(I32, (tm, tm), 1)
    before = _dot((cc < r).astype(BF16), chosen.astype(BF16)) + carry[...]
    carry[...] = carry[...] + jnp.sum(chosen, axis=0, keepdims=True)
    cnt_ref[...] = carry[...]

    idx_out = jnp.zeros((tm, LANES), F32)
    w_out = jnp.zeros((tm, LANES), F32)
    rank_out = jnp.zeros((tm, LANES), F32)
    for k in range(TOP_K):
        sel = lane == float(k)
        rk = jnp.sum(jnp.where(onehots[k], before, 0.0), axis=1, keepdims=True)
        idx_out = jnp.where(sel, idxs[k], idx_out)
        w_out = jnp.where(sel, es[k] / denom, w_out)
        rank_out = jnp.where(sel, rk, rank_out)
    idx_ref[...] = idx_out.astype(I32)
    w_ref[...] = w_out
    rank_ref[...] = rank_out.astype(I32)


def _router(x2, g, scale, shift, w_router, b_router, seq):
    t, d = x2.shape
    tm = min(seq, 256)
    per_b = seq // tm
    wr = jnp.zeros((d, LANES), F32).at[:, :N_EXPERTS].set(w_router)
    br = jnp.zeros((1, LANES), F32).at[0, :N_EXPERTS].set(b_router)
    bvec = pl.BlockSpec((1, 1, d), lambda i: (i // per_b, 0, 0))
    lane_rows = pl.BlockSpec((tm, LANES), lambda i: (i, 0))
    return pl.pallas_call(
        _router_kernel,
        out_shape=(jax.ShapeDtypeStruct((t * ROW_TILE, LANES), U32),
                   jax.ShapeDtypeStruct((t, LANES), I32),
                   jax.ShapeDtypeStruct((t, LANES), F32),
                   jax.ShapeDtypeStruct((t, LANES), I32),
                   jax.ShapeDtypeStruct((1, LANES), F32)),
        grid=(t // tm,),
        in_specs=[pl.BlockSpec((tm, d), lambda i: (i, 0)),
                  pl.BlockSpec((1, d), lambda i: (0, 0)),
                  bvec, bvec,
                  pl.BlockSpec((d, LANES), lambda i: (0, 0)),
                  pl.BlockSpec((1, LANES), lambda i: (0, 0))],
        out_specs=(pl.BlockSpec((tm * ROW_TILE, LANES), lambda i: (i, 0)),
                   lane_rows, lane_rows, lane_rows,
                   pl.BlockSpec((1, LANES), lambda i: (0, 0))),
        scratch_shapes=[pltpu.VMEM((1, LANES), F32)],
        compiler_params=_cparams(("arbitrary",), 48),
        name="router_topk",
    )(x2, g.reshape(1, d), scale, shift, wr, br)


EXPERT_TM = 256


def _plan_kernel(cnt_ref, idx_ref, rank_ref, pos_ref, te_ref, tv_ref, *, n_tiles_pad):
    tm = idx_ref.shape[0]
    cnt = cnt_ref[...]
    ntile = jnp.floor((cnt + (EXPERT_TM - 1)) * (1.0 / EXPERT_TM))
    ntile8 = jnp.broadcast_to(ntile, (SUBLANES, LANES)).astype(BF16)
    r = lax.broadcasted_iota(I32, (LANES, LANES), 0)
    c = lax.broadcasted_iota(I32, (LANES, LANES), 1)
    tile_start = _dot(ntile8, (r < c).astype(BF16))
    row_start = tile_start[0:1, :] * float(EXPERT_TM)

    lane = lax.broadcasted_iota(I32, (tm, LANES), 1)
    idx = idx_ref[...]
    rank = rank_ref[...].astype(F32)
    pos = jnp.zeros((tm, LANES), F32)
    for k in range(TOP_K):
        sel = lane == idx[:, k:k + 1]
        st = jnp.sum(jnp.where(sel, row_start, 0.0), axis=1, keepdims=True)
        pos = jnp.where(lane == k, st + rank[:, k:k + 1], pos)
    pos_ref[...] = pos.astype(I32)

    tile_end_row = tile_start[0:1, :] + ntile
    tile_end = jnp.broadcast_to(tile_end_row, (LANES, LANES)).T[:, 0:1]
    e_ok = lax.broadcasted_iota(I32, (LANES, n_tiles_pad), 0) < N_EXPERTS
    jj = lax.broadcasted_iota(I32, (LANES, n_tiles_pad), 1).astype(F32)
    done = jnp.sum(jnp.where(e_ok & (tile_end <= jj), 1.0, 0.0), axis=0, keepdims=True)
    te_ref[...] = jnp.minimum(done, float(N_EXPERTS - 1)).astype(I32)
    tv_ref[...] = (done < float(N_EXPERTS)).astype(I32)


def _plan(counts, top_idx, rank, n_tiles):
    t = top_idx.shape[0]
    tm = min(t, 1024)
    n_tiles_pad = -(-n_tiles // LANES) * LANES
    lane_rows = pl.BlockSpec((tm, LANES), lambda i: (i, 0))
    meta = pl.BlockSpec((1, n_tiles_pad), lambda i: (0, 0))
    return pl.pallas_call(
        functools.partial(_plan_kernel, n_tiles_pad=n_tiles_pad),
        out_shape=(jax.ShapeDtypeStruct((t, LANES), I32),
                   jax.ShapeDtypeStruct((1, n_tiles_pad), I32),
                   jax.ShapeDtypeStruct((1, n_tiles_pad), I32)),
        grid=(t // tm,),
        in_specs=[pl.BlockSpec((1, LANES), lambda i: (0, 0)), lane_rows, lane_rows],
        out_specs=(lane_rows, meta, meta),
        compiler_params=_cparams(("arbitrary",), 32),
        name="moe_plan",
    )(counts, top_idx, rank)


DISPATCH_TM = 256


def _dispatch_kernel(pos_hbm, h_ref, xs_in, xs_out, pos_smem, sem_idx, sem_rows):
    del xs_in
    i = pl.program_id(0)
    n = DISPATCH_TM * TOP_K
    idx_copy = pltpu.make_async_copy(pos_hbm.at[pl.ds(pl.multiple_of(i * n, n), n)], pos_smem, sem_idx)
    idx_copy.start()
    idx_copy.wait()

    def row_copy(t, slot):
        src = h_ref.at[pl.ds(pl.multiple_of(t * ROW_TILE, ROW_TILE), ROW_TILE), :]
        dst = xs_out.at[pl.ds(pl.multiple_of(slot * ROW_TILE, ROW_TILE), ROW_TILE), :]
        return pltpu.make_async_copy(src, dst, sem_rows)

    def issue(t, carry):
        for k in range(TOP_K):
            row_copy(t, pos_smem[t * TOP_K + k]).start()
        return carry

    lax.fori_loop(0, DISPATCH_TM, issue, 0)

    def drain(t, carry):
        for k in range(TOP_K):
            row_copy(t, pos_smem[t * TOP_K + k]).wait()
        return carry

    lax.fori_loop(0, DISPATCH_TM, drain, 0)


def _dispatch(pos_flat, h_rows, n_rows):
    t = h_rows.shape[0] // ROW_TILE
    xs0 = jnp.zeros((n_rows * ROW_TILE, LANES), U32)
    return pl.pallas_call(
        _dispatch_kernel,
        out_shape=jax.ShapeDtypeStruct(xs0.shape, U32),
        grid=(t // DISPATCH_TM,),
        in_specs=[pl.BlockSpec(memory_space=pl.ANY),
                  pl.BlockSpec((DISPATCH_TM * ROW_TILE, LANES), lambda i: (i, 0)),
                  pl.BlockSpec(memory_space=pl.ANY)],
        out_specs=pl.BlockSpec(memory_space=pl.ANY),
        scratch_shapes=[pltpu.SMEM((DISPATCH_TM * TOP_K,), I32),
                        pltpu.SemaphoreType.DMA, pltpu.SemaphoreType.DMA],
        input_output_aliases={2: 0},
        compiler_params=_cparams(("arbitrary",), 32),
        name="moe_dispatch",
    )(pos_flat, h_rows, xs0)


def _expert_kernel(te_ref, tv_ref, xs_ref, wgu_ref, bgu_ref, wd_ref, bd_ref, o_ref):
    j = pl.program_id(0)
    tm = EXPERT_TM
    f = wd_ref.shape[1]

    @pl.when(tv_ref[j] != 0)
    def _():
        x = _load_packed_rows(xs_ref, 0, tm).astype(BF16)
        gu = _dot(x, wgu_ref[0]) + bgu_ref[0]
        gate = jnp.minimum(gu[:, :f], SWIGLU_LIMIT)
        up = jnp.clip(gu[:, f:], -SWIGLU_LIMIT, SWIGLU_LIMIT)
        act = (up + 1.0) * (gate * _sigmoid(SWIGLU_ALPHA * gate))
        out = _dot(act.astype(BF16), wd_ref[0]) + bd_ref[0]
        _store_packed_rows(o_ref, out)

    @pl.when(tv_ref[j] == 0)
    def _():
        o_ref[...] = jnp.zeros_like(o_ref)


def _experts(tile_expert, tile_valid, xs, wgu, bgu, wd, bd, n_tiles):
    n_e, d, f2 = wgu.shape
    f = f2 // 2
    rows = pl.BlockSpec((EXPERT_TM * ROW_TILE, LANES), lambda j, te, tv: (j, 0))
    return pl.pallas_call(
        _expert_kernel,
        out_shape=jax.ShapeDtypeStruct(xs.shape, U32),
        grid_spec=pltpu.PrefetchScalarGridSpec(
            num_scalar_prefetch=2,
            grid=(n_tiles,),
            in_specs=[rows,
                      pl.BlockSpec((1, d, f2), lambda j, te, tv: (te[j], 0, 0)),
                      pl.BlockSpec((1, 1, f2), lambda j, te, tv: (te[j], 0, 0)),
                      pl.BlockSpec((1, f, d), lambda j, te, tv: (te[j], 0, 0)),
                      pl.BlockSpec((1, 1, d), lambda j, te, tv: (te[j], 0, 0))],
            out_specs=rows),
        compiler_params=_cparams(("arbitrary",), 56),
        name="moe_experts",
    )(tile_expert, tile_valid, xs, wgu, bgu.reshape(n_e, 1, f2), wd, bd.reshape(n_e, 1, d))


COMBINE_TM = 128


def _combine_kernel(pos_hbm, ys_hbm, w_ref, x_ref, g_ref, o_ref, pos_smem, buf, sem_idx, sem_rows):
    i = pl.program_id(0)
    tm = COMBINE_TM
    n = 2 * tm * TOP_K
    half = i % 2
    idx_copy = pltpu.make_async_copy(pos_hbm.at[pl.ds(pl.multiple_of((i // 2) * n, n), n)], pos_smem, sem_idx)
    idx_copy.start()
    idx_copy.wait()

    def row_copy(t, k):
        slot = pos_smem[(half * tm + t) * TOP_K + k]
        src = ys_hbm.at[pl.ds(pl.multiple_of(slot * ROW_TILE, ROW_TILE), ROW_TILE), :]
        dst = buf.at[pl.ds(pl.multiple_of((k * tm + t) * ROW_TILE, ROW_TILE), ROW_TILE), :]
        return pltpu.make_async_copy(src, dst, sem_rows)

    def issue(t, carry):
        for k in range(TOP_K):
            row_copy(t, k).start()
        return carry

    lax.fori_loop(0, tm, issue, 0)

    def drain(t, carry):
        for k in range(TOP_K):
            row_copy(t, k).wait()
        return carry

    lax.fori_loop(0, tm, drain, 0)

    w = w_ref[...]
    y = jnp.zeros(x_ref.shape, F32)
    for k in range(TOP_K):
        y = y + w[:, k:k + 1] * _load_packed_rows(buf, k * tm, tm)
    o_ref[...] = x_ref[...] + g_ref[0] * y


def _combine(pos_flat, ys, top_w, x2, gate, seq):
    t, d = x2.shape
    tm = COMBINE_TM
    per_b = seq // tm
    return pl.pallas_call(
        _combine_kernel,
        out_shape=jax.ShapeDtypeStruct((t, d), F32),
        grid=(t // tm,),
        in_specs=[pl.BlockSpec(memory_space=pl.ANY),
                  pl.BlockSpec(memory_space=pl.ANY),
                  pl.BlockSpec((tm, LANES), lambda i: (i, 0)),
                  pl.BlockSpec((tm, d), lambda i: (i, 0)),
                  pl.BlockSpec((1, 1, d), lambda i: (i // per_b, 0, 0))],
        out_specs=pl.BlockSpec((tm, d), lambda i: (i, 0)),
        scratch_shapes=[pltpu.SMEM((2 * tm * TOP_K,), I32),
                        pltpu.VMEM((TOP_K * tm * ROW_TILE, LANES), U32),
                        pltpu.SemaphoreType.DMA, pltpu.SemaphoreType.DMA],
        compiler_params=_cparams(("arbitrary",), 32),
        name="moe_combine",
    )(pos_flat, ys, top_w, x2, gate)


def _final_norm_kernel(x_ref, g_ref, o_ref):
    x = x_ref[...]
    o_ref[...] = x * lax.rsqrt(jnp.mean(x * x, axis=-1, keepdims=True) + RMS_EPS) * g_ref[...]


def _final_norm(x2, g):
    t, d = x2.shape
    tm = min(t, 512)
    return pl.pallas_call(
        _final_norm_kernel,
        out_shape=jax.ShapeDtypeStruct((t, d), F32),
        grid=(t // tm,),
        in_specs=[pl.BlockSpec((tm, d), lambda i: (i, 0)), pl.BlockSpec((1, d), lambda i: (0, 0))],
        out_specs=pl.BlockSpec((tm, d), lambda i: (i, 0)),
        compiler_params=_cparams(("parallel",), 32),
        name="final_norm",
    )(x2, g.reshape(1, d))


def _spread_rope_cols(w):
    half = MLA_ROPE_DIM // 2
    z = jnp.zeros(w.shape[:-1] + (half,), w.dtype)
    return jnp.concatenate([w[..., :half], z, w[..., half:], z], axis=-1)


def _layout_w_in(w_in, d):
    sizes = (SB_W, SB_W, SB_W, FOX_W, FOX_W, FOX_W, FOX_HEADS, MLA_Q_RANK, MLA_KV_RANK, MLA_ROPE_DIM,
             N_BRANCH * d)
    offs = np.concatenate([[0], np.cumsum(sizes)])
    seg = lambda k: w_in[:, int(offs[k]):int(offs[k + 1])]
    ff = jnp.zeros((w_in.shape[0], LANES), w_in.dtype).at[:, :FOX_HEADS].set(seg(6))
    cols = [seg(0), seg(1), seg(2), seg(3), seg(4), seg(5), seg(7), seg(8), _spread_rope_cols(seg(9)), ff,
            seg(10)]
    return jnp.concatenate(cols, axis=1).astype(BF16)


def _layout_w_q_up(w):
    r = w.shape[0]
    w3 = w.reshape(r, MLA_HEADS, MLA_QK_DIM)
    nope = w3[:, :, :MLA_NOPE_DIM].reshape(r, MLA_HEADS * MLA_NOPE_DIM)
    rope = _spread_rope_cols(w3[:, :, MLA_NOPE_DIM:]).reshape(r, MLA_HEADS * LANES)
    return jnp.concatenate([nope, rope], axis=1).astype(BF16)


def _layout_w_kv_up(w):
    r = w.shape[0]
    w3 = w.reshape(r, MLA_HEADS, MLA_NOPE_DIM + MLA_V_DIM)
    kn = w3[:, :, :MLA_NOPE_DIM].reshape(r, MLA_HEADS * MLA_NOPE_DIM)
    v = w3[:, :, MLA_NOPE_DIM:].reshape(r, MLA_HEADS * MLA_V_DIM)
    return jnp.concatenate([kn, v], axis=1).astype(BF16)


def kernel(x, c, positions, w_ada, b_ada, g_mix_norm, g_ffn_norm, w_in, b_forget, g_q_norm, g_kv_norm,
           w_q_up, w_kv_up, w_sb_proj, w_fox_proj, w_mla_proj, w_out, w_router, b_router, w_gate_up,
           b_gate_up, w_down, b_down, g_final):
    batch, seq, d = x.shape
    t = batch * seq
    depth = w_ada.shape[0]
    n_rows = t * TOP_K + N_EXPERTS * EXPERT_TM
    n_tiles = n_rows // EXPERT_TM

    mod = _adaln(c, w_ada, b_ada)
    cos_t, sin_t = _rope_tables(positions)
    x2 = x.reshape(t, d)

    for layer in range(depth):
        mvec = lambda k: mod[layer, :, k].reshape(batch, 1, d)
        proj, aux = _premix(x2, g_mix_norm[layer], mvec(1), mvec(0), _layout_w_in(w_in[layer], d), seq)
        qn, qr, kn, v, kr = _mla_prep(proj, cos_t, sin_t, g_q_norm[layer], g_kv_norm[layer],
                                      _layout_w_q_up(w_q_up[layer]), _layout_w_kv_up(w_kv_up[layer]))
        frep, ft = _fox_gates(aux, b_forget[layer], batch, seq)
        y_sb = _sb_attention(proj, batch, seq)
        y_fox = _fox_attention(proj, frep, ft, batch, seq)
        y_mla = _mla_attention(qn, qr, kn, kr, v, batch, seq)
        merged = _merge(y_sb, y_fox, y_mla, w_sb_proj[layer].astype(BF16), w_fox_proj[layer].astype(BF16),
                        w_mla_proj[layer].astype(BF16), proj, d)
        x2 = _outproj(merged, w_out[layer].astype(BF16), x2, mvec(2), seq)

        h_rows, top_idx, top_w, rank, counts = _router(x2, g_ffn_norm[layer], mvec(4), mvec(3),
                                                       w_router[layer], b_router[layer], seq)
        pos, tile_expert, tile_valid = _plan(counts, top_idx, rank, n_tiles)
        pos_flat = pos[:, :TOP_K].reshape(t * TOP_K)
        xs = _dispatch(pos_flat, h_rows, n_rows)
        ys = _experts(tile_expert[0], tile_valid[0], xs, w_gate_up[layer].astype(BF16), b_gate_up[layer],
                      w_down[layer].astype(BF16), b_down[layer], n_tiles)
        x2 = _combine(pos_flat, ys, top_w, x2, mvec(5), seq)

    return _final_norm(x2, g_final).reshape(batch, seq, d)
```

```python
import functools

import numpy as np
import jax
import jax.numpy as jnp
from jax import lax
from jax.experimental import pallas as pl
from jax.experimental.pallas import tpu as pltpu

F32 = jnp.float32
BF16 = jnp.bfloat16
I32 = jnp.int32
U32 = jnp.uint32

HEAD_DIM = 128
SB_HEADS = 4
FOX_HEADS = 4
MLA_HEADS = 8
SB_W = SB_HEADS * HEAD_DIM
FOX_W = FOX_HEADS * HEAD_DIM
MLA_Q_RANK = 512
MLA_KV_RANK = 256
MLA_NOPE_DIM = 128
MLA_ROPE_DIM = 64
MLA_V_DIM = 128
MLA_QK_DIM = MLA_NOPE_DIM + MLA_ROPE_DIM
MLA_W = MLA_HEADS * MLA_V_DIM
ROPE_THETA = 10000.0
N_BRANCH = 3
RMS_EPS = 1e-6
N_EXPERTS = 32
TOP_K = 4
SWIGLU_LIMIT = 7.0
SWIGLU_ALPHA = 1.702
N_MOD = 6

LANES = 128
SUBLANES = 8
MIB = 1024 * 1024
NEG_BIG = -1e30

NORM_ROWS = 256

PROJ_TN = 512
T_SBQ, T_SBK, T_SBV, T_FXQ, T_FXK, T_FXV, T_CQ, T_AUX, T_GATE = 0, 1, 2, 3, 4, 5, 6, 7, 8
AUX_KR = MLA_KV_RANK
AUX_FF = MLA_KV_RANK + LANES


def _cparams(sem, vmem_mib):
    return pltpu.CompilerParams(dimension_semantics=sem, vmem_limit_bytes=vmem_mib * MIB)


def _sigmoid(x):
    return 1.0 / (1.0 + jnp.exp(-x))


def _split2(x):
    hi = x.astype(BF16)
    lo = (x - hi.astype(F32)).astype(BF16)
    return hi, lo


def _split3(x):
    p1 = x.astype(BF16)
    r1 = x - p1.astype(F32)
    p2 = r1.astype(BF16)
    r2 = r1 - p2.astype(F32)
    return p1, p2, r2.astype(BF16)


def _dot(a, b):
    return jnp.dot(a, b, preferred_element_type=F32)


def _dot_nt(a, b):
    return lax.dot_general(a, b, (((1,), (1,)), ((), ())), preferred_element_type=F32)


def _rope(x, c, s):
    return x * c + pltpu.roll(x, LANES // 2, 1) * s


def _pack_bf16_pair(a, b):
    ua = lax.bitcast_convert_type(a.astype(BF16).astype(F32), U32)
    ub = lax.bitcast_convert_type(b.astype(BF16).astype(F32), U32)
    return (ua >> 16) | (ub & jnp.uint32(0xFFFF0000))


def _unpack_bf16_pair(w):
    a = lax.bitcast_convert_type(w << 16, F32)
    b = lax.bitcast_convert_type(w & jnp.uint32(0xFFFF0000), F32)
    return a, b


def _adaln_kernel(c_ref, w_ref, b_ref, o_ref):
    c = c_ref[...]
    cond = c * _sigmoid(c)
    c1, c2 = _split2(cond)
    w1, w2 = _split2(w_ref[0])
    o_ref[0] = _dot(c1, w1) + _dot(c1, w2) + _dot(c2, w1) + b_ref[0]


def _adaln(c, w_ada, b_ada):
    n_layer, d, n6 = w_ada.shape
    b = c.shape[0]
    tn = 1024
    c8 = jnp.zeros((SUBLANES, d), F32).at[:b].set(c)
    out = pl.pallas_call(
        _adaln_kernel,
        out_shape=jax.ShapeDtypeStruct((n_layer, SUBLANES, n6), F32),
        grid=(n_layer, n6 // tn),
        in_specs=[
            pl.BlockSpec((SUBLANES, d), lambda l, n: (0, 0)),
            pl.BlockSpec((1, d, tn), lambda l, n: (l, 0, n)),
            pl.BlockSpec((1, 1, tn), lambda l, n: (l, 0, n)),
        ],
        out_specs=pl.BlockSpec((1, SUBLANES, tn), lambda l, n: (l, 0, n)),
        compiler_params=_cparams(("parallel", "parallel"), 48),
        name="adaln",
    )(c8, w_ada, b_ada.reshape(n_layer, 1, n6))
    return out[:, :b].reshape(n_layer, b, N_MOD, d)


def _rope_kernel(pos_ref, invf_ref, cm_ref, sg_ref, c_ref, s_ref):
    ang = pos_ref[...].astype(F32) * invf_ref[...]
    c_ref[...] = jnp.cos(ang) * cm_ref[...]
    s_ref[...] = jnp.sin(ang) * sg_ref[...]


def _rope_tables(positions):
    t = positions.size
    half = MLA_ROPE_DIM // 2
    inv_freq = ROPE_THETA ** (-jnp.arange(half, dtype=F32) / half)
    z = jnp.zeros((half,), F32)
    o = jnp.ones((half,), F32)
    invf = jnp.concatenate([inv_freq, z, inv_freq, z]).reshape(1, LANES)
    cmask = jnp.concatenate([o, z, o, z]).reshape(1, LANES)
    sgn = jnp.concatenate([-o, z, o, z]).reshape(1, LANES)
    tm = min(t, 1024)
    row = pl.BlockSpec((1, LANES), lambda i: (0, 0))
    return pl.pallas_call(
        _rope_kernel,
        out_shape=(jax.ShapeDtypeStruct((t, LANES), F32),) * 2,
        grid=(t // tm,),
        in_specs=[pl.BlockSpec((tm, 1), lambda i: (i, 0)), row, row, row],
        out_specs=(pl.BlockSpec((tm, LANES), lambda i: (i, 0)),) * 2,
        compiler_params=_cparams(("parallel",), 32),
        name="rope_tables",
    )(positions.reshape(t, 1), invf, cmask, sgn)


def _premix_kernel(x_ref, g_ref, sc_ref, sh_ref, w_ref, o_ref, aux_ref, h_scr):
    n = pl.program_id(1)

    @pl.when(n == 0)
    def _():
        def norm_rows(r, carry):
            rows = pl.ds(pl.multiple_of(r * NORM_ROWS, NORM_ROWS), NORM_ROWS)
            x = x_ref[rows, :]
            y = x * lax.rsqrt(jnp.mean(x * x, axis=-1, keepdims=True) + RMS_EPS) * g_ref[...]
            h_scr[rows, :] = (y * (1.0 + sc_ref[0]) + sh_ref[0]).astype(BF16)
            return carry

        lax.fori_loop(0, x_ref.shape[0] // NORM_ROWS, norm_rows, 0)

    acc = _dot(h_scr[...], w_ref[...])
    o_ref[...] = acc.astype(BF16)

    @pl.when(n == T_AUX)
    def _():
        aux_ref[...] = acc[:, AUX_FF:AUX_FF + LANES]


def _premix(x2, g, scale, shift, w1_all, layer, seq):
    t, d = x2.shape
    n1 = w1_all.shape[2]
    tm = min(seq, 1024)
    per_b = seq // tm
    bvec = pl.BlockSpec((1, 1, d), lambda i, n: (i // per_b, 0, 0))
    return pl.pallas_call(
        _premix_kernel,
        out_shape=(jax.ShapeDtypeStruct((t, n1), BF16), jax.ShapeDtypeStruct((t, LANES), F32)),
        grid=(t // tm, n1 // PROJ_TN),
        in_specs=[
            pl.BlockSpec((tm, d), lambda i, n: (i, 0)),
            pl.BlockSpec((1, d), lambda i, n: (0, 0)),
            bvec, bvec,
            pl.BlockSpec((None, d, PROJ_TN), lambda i, n: (layer, 0, n)),
        ],
        out_specs=(pl.BlockSpec((tm, PROJ_TN), lambda i, n: (i, n)),
                   pl.BlockSpec((tm, LANES), lambda i, n: (i, 0))),
        scratch_shapes=[pltpu.VMEM((tm, d), BF16)],
        compiler_params=_cparams(("parallel", "arbitrary"), 48),
        name="premix_proj",
    )(x2, g.reshape(1, d), scale, shift, w1_all)


def _mla_prep_kernel(cq_ref, t7_ref, c_ref, s_ref, gq_ref, gkv_ref, wq_ref, wkv_ref,
                     qn_ref, qr_ref, kn_ref, v_ref, kr_ref):
    cq = cq_ref[...].astype(F32)
    cqn = cq * lax.rsqrt(jnp.mean(cq * cq, axis=-1, keepdims=True) + RMS_EPS) * gq_ref[...]
    q = _dot(cqn.astype(BF16), wq_ref[...]) * (MLA_QK_DIM ** -0.5)
    nw = MLA_HEADS * MLA_NOPE_DIM
    qn_ref[...] = q[:, :nw].astype(BF16)
    c = c_ref[...]
    s = s_ref[...]
    for h in range(MLA_HEADS):
        xr = q[:, nw + h * LANES: nw + (h + 1) * LANES]
        qr_ref[:, h * LANES:(h + 1) * LANES] = _rope(xr, c, s).astype(BF16)
    t7 = t7_ref[...].astype(F32)
    ckv = t7[:, :MLA_KV_RANK]
    ckvn = ckv * lax.rsqrt(jnp.mean(ckv * ckv, axis=-1, keepdims=True) + RMS_EPS) * gkv_ref[...]
    kv = _dot(ckvn.astype(BF16), wkv_ref[...])
    kn_ref[...] = kv[:, :nw].astype(BF16)
    v_ref[...] = kv[:, nw:].astype(BF16)
    kr_ref[...] = _rope(t7[:, AUX_KR:AUX_KR + LANES], c, s).astype(BF16)


def _mla_prep(proj, cos_t, sin_t, gq, gkv, wq, wkv):
    t = proj.shape[0]
    tm = min(t, 512)
    nw = MLA_HEADS * MLA_NOPE_DIM
    full = lambda shape: pl.BlockSpec(shape, lambda i: (0, 0))
    rows = lambda w: pl.BlockSpec((tm, w), lambda i: (i, 0))
    return pl.pallas_call(
        _mla_prep_kernel,
        out_shape=(jax.ShapeDtypeStruct((t, nw), BF16),) * 4 + (jax.ShapeDtypeStruct((t, LANES), BF16),),
        grid=(t // tm,),
        in_specs=[
            pl.BlockSpec((tm, PROJ_TN), lambda i: (i, T_CQ)),
            pl.BlockSpec((tm, PROJ_TN), lambda i: (i, T_AUX)),
            rows(LANES), rows(LANES),
            full((1, MLA_Q_RANK)), full((1, MLA_KV_RANK)),
            full(wq.shape), full(wkv.shape),
        ],
        out_specs=(rows(nw),) * 4 + (rows(LANES),),
        compiler_params=_cparams(("parallel",), 48),
        name="mla_prep",
    )(proj, proj, cos_t, sin_t, gq.reshape(1, -1), gkv.reshape(1, -1), wq, wkv)


def _fox_gate_kernel(aux_ref, b_ref, frep_ref, ft_ref, carry):
    j = pl.program_id(1)
    tm = aux_ref.shape[0]

    @pl.when(j == 0)
    def _():
        carry[...] = jnp.zeros_like(carry)

    xx = aux_ref[...] + b_ref[...]
    lf = jnp.minimum(xx, 0.0) - jnp.log1p(jnp.exp(-jnp.abs(xx)))
    r = lax.broadcasted_iota(I32, (tm, tm), 0)
    cc = lax.broadcasted_iota(I32, (tm, tm), 1)
    tri = (cc <= r).astype(BF16)
    p1, p2, p3 = _split3(lf)
    f = _dot(tri, p1) + _dot(tri, p2) + _dot(tri, p3) + carry[...]
    carry[...] = f[tm - 1:tm, :]
    ft = f.T
    for h in range(FOX_HEADS):
        frep_ref[h] = jnp.broadcast_to(f[:, h:h + 1], (tm, LANES))
        ft_ref[h, 0] = ft[h:h + 1, :]


def _fox_gates(aux, b_forget, batch, seq):
    tm = min(seq, 256)
    ns = seq // tm
    bpad = jnp.zeros((1, LANES), F32).at[0, :FOX_HEADS].set(b_forget)
    return pl.pallas_call(
        _fox_gate_kernel,
        out_shape=(jax.ShapeDtypeStruct((FOX_HEADS, batch * seq, LANES), F32),
                   jax.ShapeDtypeStruct((FOX_HEADS, batch, 1, seq), F32)),
        grid=(batch, ns),
        in_specs=[pl.BlockSpec((tm, LANES), lambda b, j: (b * ns + j, 0)),
                  pl.BlockSpec((1, LANES), lambda b, j: (0, 0))],
        out_specs=(pl.BlockSpec((FOX_HEADS, tm, LANES), lambda b, j: (0, b * ns + j, 0)),
                   pl.BlockSpec((FOX_HEADS, 1, 1, tm), lambda b, j: (0, b, 0, j))),
        scratch_shapes=[pltpu.VMEM((1, LANES), F32)],
        compiler_params=_cparams(("parallel", "arbitrary"), 32),
        name="fox_gates",
    )(aux, bpad)


def _softmax_block(s, v, m_scr, l_scr, acc_scr):
    m_prev = m_scr[...]
    m_new = jnp.maximum(m_prev, jnp.max(s, axis=1, keepdims=True))
    alpha = jnp.exp(m_prev - m_new)
    reps = s.shape[1] // LANES
    p = jnp.exp(s - jnp.concatenate([m_new] * reps, axis=1))
    l_scr[...] = alpha * l_scr[...] + jnp.sum(p, axis=1, keepdims=True)
    acc_scr[...] = alpha * acc_scr[...] + _dot(p.astype(BF16), v)
    m_scr[...] = m_new


def _causal_softmax_attention(i, tq, q, load_k, load_v, bias, o_ref, m_scr, l_scr, acc_scr):
    m_scr[...] = jnp.full_like(m_scr, NEG_BIG)
    l_scr[...] = jnp.zeros_like(l_scr)
    acc_scr[...] = jnp.zeros_like(acc_scr)

    row = lax.broadcasted_iota(I32, (tq, tq), 0)
    col = lax.broadcasted_iota(I32, (tq, tq), 1)
    s = bias(_dot_nt(q, load_k(i)), i)
    _softmax_block(jnp.where(col <= row, s, NEG_BIG), load_v(i), m_scr, l_scr, acc_scr)

    def body(j, carry):
        _softmax_block(bias(_dot_nt(q, load_k(j)), j), load_v(j), m_scr, l_scr, acc_scr)
        return carry

    lax.fori_loop(0, i, body, 0)
    o_ref[...] = (acc_scr[...] / l_scr[...]).astype(o_ref.dtype)


def _fox_attn_kernel(q_ref, k_ref, v_ref, fq_ref, fk_ref, o_ref, m_scr, l_scr, acc_scr):
    i = pl.program_id(2)
    tq = q_ref.shape[0]
    q = (q_ref[...].astype(F32) * (HEAD_DIM ** -0.5)).astype(BF16)
    fq = fq_ref[0]
    reps = tq // LANES

    def blk(j):
        return pl.ds(pl.multiple_of(j * tq, tq), tq)

    def bias(s, j):
        return s + jnp.concatenate([fq] * reps, axis=1) - fk_ref[0, 0, :, blk(j)]

    _causal_softmax_attention(i, tq, q, lambda j: k_ref[blk(j), :], lambda j: v_ref[blk(j), :],
                              bias, o_ref, m_scr, l_scr, acc_scr)


def _attn_tq(seq):
    return min(seq, 512)


def _attn_scratch(tq):
    return [pltpu.VMEM((tq, LANES), F32)] * 3


def _fox_attention(proj, frep, ft, batch, seq):
    tq = _attn_tq(seq)
    nq = seq // tq
    t = batch * seq
    qspec = lambda base: pl.BlockSpec((tq, HEAD_DIM), lambda b, h, i: (b * nq + i, base + h))
    kvspec = lambda base: pl.BlockSpec((seq, HEAD_DIM), lambda b, h, i: (b, base + h))
    per_tile = PROJ_TN // HEAD_DIM
    return pl.pallas_call(
        _fox_attn_kernel,
        out_shape=jax.ShapeDtypeStruct((t, FOX_W), BF16),
        grid=(batch, FOX_HEADS, nq),
        in_specs=[qspec(T_FXQ * per_tile), kvspec(T_FXK * per_tile), kvspec(T_FXV * per_tile),
                  pl.BlockSpec((1, tq, LANES), lambda b, h, i: (h, b * nq + i, 0)),
                  pl.BlockSpec((1, 1, 1, seq), lambda b, h, i: (h, b, 0, 0))],
        out_specs=pl.BlockSpec((tq, HEAD_DIM), lambda b, h, i: (b * nq + i, h)),
        scratch_shapes=_attn_scratch(tq),
        compiler_params=_cparams(("parallel", "parallel", "arbitrary"), 48),
        name="fox_attention",
    )(proj, proj, proj, frep, ft)


def _mla_attn_kernel(qn_ref, qr_ref, kn_ref, kr_ref, v_ref, o_ref, m_scr, l_scr, acc_scr):
    i = pl.program_id(2)
    tq = qn_ref.shape[0]
    q = jnp.concatenate([qn_ref[...], qr_ref[...]], axis=1)

    def blk(j):
        return pl.ds(pl.multiple_of(j * tq, tq), tq)

    def load_k(j):
        return jnp.concatenate([kn_ref[blk(j), :], kr_ref[blk(j), :]], axis=1)

    _causal_softmax_attention(i, tq, q, load_k, lambda j: v_ref[blk(j), :],
                              lambda s, j: s, o_ref, m_scr, l_scr, acc_scr)


def _mla_attention(qn, qr, kn, kr, v, batch, seq):
    tq = _attn_tq(seq)
    nq = seq // tq
    t = batch * seq
    qspec = pl.BlockSpec((tq, HEAD_DIM), lambda b, h, i: (b * nq + i, h))
    kvspec = pl.BlockSpec((seq, HEAD_DIM), lambda b, h, i: (b, h))
    return pl.pallas_call(
        _mla_attn_kernel,
        out_shape=jax.ShapeDtypeStruct((t, MLA_W), BF16),
        grid=(batch, MLA_HEADS, nq),
        in_specs=[qspec, qspec, kvspec,
                  pl.BlockSpec((seq, LANES), lambda b, h, i: (b, 0)),
                  kvspec],
        out_specs=qspec,
        scratch_shapes=_attn_scratch(tq),
        compiler_params=_cparams(("parallel", "parallel", "arbitrary"), 48),
        name="mla_attention",
    )(qn, qr, kn, kr, v)


SB_TK = 128


def _sb_attn_kernel(q_ref, k_ref, v_ref, o_ref, run_scr, acc_scr):
    i = pl.program_id(2)
    tq = q_ref.shape[0]
    tk = SB_TK
    n_sub = tq // tk
    q = (q_ref[...].astype(F32) * (HEAD_DIM ** -0.5)).astype(BF16)
    run_scr[...] = jnp.zeros_like(run_scr)
    acc_scr[...] = jnp.zeros_like(acc_scr)

    rj = lax.broadcasted_iota(I32, (tk, 2 * tk), 0)
    cs = lax.broadcasted_iota(I32, (tk, 2 * tk), 1)
    later_or_all = ((cs >= tk) | (rj > cs)).astype(BF16)

    def superblock(jb, past):
        ks = pl.ds(pl.multiple_of(jb * tq, tq), tq)
        z = _dot_nt(q, k_ref[ks, :])
        log_beta = jnp.minimum(z, 0.0) - jnp.log(1.0 + jnp.exp(-jnp.abs(z)))
        log_not = log_beta - z
        if past is not None:
            log_not = jnp.where(past, log_not, 0.0)
        ln16 = log_not.astype(BF16)
        sums = [_dot(ln16[:, b * tk:(b + 1) * tk], later_or_all) for b in range(n_sub)]
        after = run_scr[...]
        offs = [None] * n_sub
        for b in reversed(range(n_sub)):
            offs[b] = sums[b][:, :tk] + after
            after = after + sums[b][:, tk:]
        run_scr[...] = after
        a = jnp.exp(log_beta + jnp.concatenate(offs, axis=1))
        if past is not None:
            a = jnp.where(past, a, 0.0)
        acc_scr[...] += _dot(a.astype(BF16), v_ref[ks, :])

    qpos = lax.broadcasted_iota(I32, (tq, tq), 0)
    kpos = lax.broadcasted_iota(I32, (tq, tq), 1)
    superblock(i, kpos < qpos)

    def body(it, carry):
        superblock(i - 1 - it, None)
        return carry

    lax.fori_loop(0, i, body, 0)
    o_ref[...] = acc_scr[...].astype(o_ref.dtype)


def _sb_attention(proj, batch, seq):
    tq = _attn_tq(seq)
    nq = seq // tq
    t = batch * seq
    per_tile = PROJ_TN // HEAD_DIM
    qspec = lambda base: pl.BlockSpec((tq, HEAD_DIM), lambda b, h, i: (b * nq + i, base + h))
    kvspec = lambda base: pl.BlockSpec((seq, HEAD_DIM), lambda b, h, i: (b, base + h))
    return pl.pallas_call(
        _sb_attn_kernel,
        out_shape=jax.ShapeDtypeStruct((t, SB_W), BF16),
        grid=(batch, SB_HEADS, nq),
        in_specs=[qspec(T_SBQ * per_tile), kvspec(T_SBK * per_tile), kvspec(T_SBV * per_tile)],
        out_specs=pl.BlockSpec((tq, HEAD_DIM), lambda b, h, i: (b * nq + i, h)),
        scratch_shapes=[pltpu.VMEM((tq, SB_TK), F32), pltpu.VMEM((tq, HEAD_DIM), F32)],
        compiler_params=_cparams(("parallel", "parallel", "arbitrary"), 48),
        name="sb_attention",
    )(proj, proj, proj)


def _merge_kernel(ysb_ref, yfx_ref, yml_ref, wsb_ref, wfx_ref, wml_ref, gsb_ref, gfx_ref, gml_ref, o_ref):
    def term(g_ref, y_ref, w_ref):
        return _sigmoid(g_ref[...].astype(F32)) * _dot(y_ref[...], w_ref[...])

    o_ref[...] = (term(gsb_ref, ysb_ref, wsb_ref) + term(gfx_ref, yfx_ref, wfx_ref)
                  + term(gml_ref, yml_ref, wml_ref)).astype(BF16)


def _merge(y_sb, y_fox, y_mla, w_sb, w_fox, w_mla, proj, d):
    t = y_sb.shape[0]
    tm = min(t, 1024)
    tn = PROJ_TN
    nd = d // tn
    ys = lambda w: pl.BlockSpec((tm, w), lambda i, n: (i, 0))
    ws = lambda k: pl.BlockSpec((k, tn), lambda i, n: (0, n))
    gs = lambda br: pl.BlockSpec((tm, tn), lambda i, n: (i, T_GATE + br * nd + n))
    return pl.pallas_call(
        _merge_kernel,
        out_shape=jax.ShapeDtypeStruct((t, d), BF16),
        grid=(t // tm, nd),
        in_specs=[ys(SB_W), ys(FOX_W), ys(MLA_W), ws(SB_W), ws(FOX_W), ws(MLA_W), gs(0), gs(1), gs(2)],
        out_specs=pl.BlockSpec((tm, tn), lambda i, n: (i, n)),
        compiler_params=_cparams(("parallel", "parallel"), 48),
        name="merge",
    )(y_sb, y_fox, y_mla, w_sb, w_fox, w_mla, proj, proj, proj)


def _outproj_kernel(m_ref, w_ref, x_ref, g_ref, o_ref):
    o_ref[...] = x_ref[...] + g_ref[0] * _dot(m_ref[...], w_ref[...])


def _outproj(merged, w_out, x2, gate, seq):
    t, d = x2.shape
    tm = min(seq, 1024)
    per_b = seq // tm
    tn = PROJ_TN
    return pl.pallas_call(
        _outproj_kernel,
        out_shape=jax.ShapeDtypeStruct((t, d), F32),
        grid=(t // tm, d // tn),
        in_specs=[pl.BlockSpec((tm, d), lambda i, n: (i, 0)),
                  pl.BlockSpec((d, tn), lambda i, n: (0, n)),
                  pl.BlockSpec((tm, tn), lambda i, n: (i, n)),
                  pl.BlockSpec((1, 1, tn), lambda i, n: (i // per_b, 0, n))],
        out_specs=pl.BlockSpec((tm, tn), lambda i, n: (i, n)),
        compiler_params=_cparams(("parallel", "parallel"), 48),
        name="outproj",
    )(merged, w_out, x2, gate)


ROW_TILE = SUBLANES


def _store_packed_rows(ref, val):
    tm, d = val.shape
    half = d // 2
    packed = _pack_bf16_pair(val[:, :half], val[:, half:])
    for c in range(ROW_TILE):
        ref[pl.ds(c, tm, stride=ROW_TILE), :] = packed[:, c * LANES:(c + 1) * LANES]


def _load_packed_rows(ref, base, tm):
    los, his = [], []
    for c in range(ROW_TILE):
        lo, hi = _unpack_bf16_pair(ref[pl.ds(base * ROW_TILE + c, tm, stride=ROW_TILE), :])
        los.append(lo)
        his.append(hi)
    return jnp.concatenate(los + his, axis=1)


def _router_kernel(x_ref, g_ref, sc_ref, sh_ref, wr_ref, br_ref,
                   h_ref, idx_ref, w_ref, rank_ref, cnt_ref, carry):
    step = pl.program_id(0)
    tm = x_ref.shape[0]

    @pl.when(step == 0)
    def _():
        carry[...] = jnp.zeros_like(carry)

    x = x_ref[...]
    y = x * lax.rsqrt(jnp.mean(x * x, axis=-1, keepdims=True) + RMS_EPS) * g_ref[...]
    h = y * (1.0 + sc_ref[0]) + sh_ref[0]
    _store_packed_rows(h_ref, h)

    h1, h2 = _split2(h)
    w1, w2 = _split2(wr_ref[...])
    logits = _dot(h1, w1) + _dot(h1, w2) + _dot(h2, w1) + br_ref[...]

    lane = lax.broadcasted_iota(I32, (tm, LANES), 1).astype(F32)
    lg = jnp.where(lane < N_EXPERTS, logits, NEG_BIG)
    vals, idxs = [], []
    for _ in range(TOP_K):
        m = jnp.max(lg, axis=1, keepdims=True)
        ix = jnp.min(jnp.where(lg == m, lane, float(LANES)), axis=1, keepdims=True)
        vals.append(m)
        idxs.append(ix)
        lg = jnp.where(lane == ix, NEG_BIG, lg)

    es = [jnp.exp(v - vals[0]) for v in vals]
    denom = es[0] + es[1] + es[2] + es[3]

    onehots = [(lane == ix) for ix in idxs]
    chosen = sum(oh.astype(F32) for oh in onehots)
    r = lax.broadcasted_iota(I32, (tm, tm), 0)
    cc = lax.broadcasted_iota(I32, (tm, tm), 1)
    before = _dot((cc < r).astype(BF16), chosen.astype(BF16)) + carry[...]
    carry[...] = carry[...] + jnp.sum(chosen, axis=0, keepdims=True)
    cnt_ref[...] = carry[...]

    idx_out = jnp.zeros((tm, LANES), F32)
    w_out = jnp.zeros((tm, LANES), F32)
    rank_out = jnp.zeros((tm, LANES), F32)
    for k in range(TOP_K):
        sel = lane == float(k)
        rk = jnp.sum(jnp.where(onehots[k], before, 0.0), axis=1, keepdims=True)
        idx_out = jnp.where(sel, idxs[k], idx_out)
        w_out = jnp.where(sel, es[k] / denom, w_out)
        rank_out = jnp.where(sel, rk, rank_out)
    idx_ref[...] = idx_out.astype(I32)
    w_ref[...] = w_out
    rank_ref[...] = rank_out.astype(I32)


def _router(x2, g, scale, shift, w_router, b_router, seq):
    t, d = x2.shape
    tm = min(seq, 256)
    per_b = seq // tm
    wr = jnp.zeros((d, LANES), F32).at[:, :N_EXPERTS].set(w_router)
    br = jnp.zeros((1, LANES), F32).at[0, :N_EXPERTS].set(b_router)
    bvec = pl.BlockSpec((1, 1, d), lambda i: (i // per_b, 0, 0))
    lane_rows = pl.BlockSpec((tm, LANES), lambda i: (i, 0))
    return pl.pallas_call(
        _router_kernel,
        out_shape=(jax.ShapeDtypeStruct((t * ROW_TILE, LANES), U32),
                   jax.ShapeDtypeStruct((t, LANES), I32),
                   jax.ShapeDtypeStruct((t, LANES), F32),
                   jax.ShapeDtypeStruct((t, LANES), I32),
                   jax.ShapeDtypeStruct((1, LANES), F32)),
        grid=(t // tm,),
        in_specs=[pl.BlockSpec((tm, d), lambda i: (i, 0)),
                  pl.BlockSpec((1, d), lambda i: (0, 0)),
                  bvec, bvec,
                  pl.BlockSpec((d, LANES), lambda i: (0, 0)),
                  pl.BlockSpec((1, LANES), lambda i: (0, 0))],
        out_specs=(pl.BlockSpec((tm * ROW_TILE, LANES), lambda i: (i, 0)),
                   lane_rows, lane_rows, lane_rows,
                   pl.BlockSpec((1, LANES), lambda i: (0, 0))),
        scratch_shapes=[pltpu.VMEM((1, LANES), F32)],
        compiler_params=_cparams(("arbitrary",), 48),
        name="router_topk",
    )(x2, g.reshape(1, d), scale, shift, wr, br)


EXPERT_TM = 256


def _plan_kernel(cnt_ref, idx_ref, rank_ref, pos_ref, te_ref, tv_ref, *, n_tiles_pad):
    tm = idx_ref.shape[0]
    cnt = cnt_ref[...]
    ntile = jnp.floor((cnt + (EXPERT_TM - 1)) * (1.0 / EXPERT_TM))
    ntile8 = jnp.broadcast_to(ntile, (SUBLANES, LANES)).astype(BF16)
    r = lax.broadcasted_iota(I32, (LANES, LANES), 0)
    c = lax.broadcasted_iota(I32, (LANES, LANES), 1)
    tile_start = _dot(ntile8, (r < c).astype(BF16))
    row_start = tile_start[0:1, :] * float(EXPERT_TM)

    lane = lax.broadcasted_iota(I32, (tm, LANES), 1)
    idx = idx_ref[...]
    rank = rank_ref[...].astype(F32)
    pos = jnp.zeros((tm, LANES), F32)
    for k in range(TOP_K):
        sel = lane == idx[:, k:k + 1]
        st = jnp.sum(jnp.where(sel, row_start, 0.0), axis=1, keepdims=True)
        pos = jnp.where(lane == k, st + rank[:, k:k + 1], pos)
    pos_ref[...] = pos.astype(I32)

    tile_end_row = tile_start[0:1, :] + ntile
    tile_end = jnp.broadcast_to(tile_end_row, (LANES, LANES)).T[:, 0:1]
    e_ok = lax.broadcasted_iota(I32, (LANES, n_tiles_pad), 0) < N_EXPERTS
    jj = lax.broadcasted_iota(I32, (LANES, n_tiles_pad), 1).astype(F32)
    done = jnp.sum(jnp.where(e_ok & (tile_end <= jj), 1.0, 0.0), axis=0, keepdims=True)
    te_ref[...] = jnp.minimum(done, float(N_EXPERTS - 1)).astype(I32)
    tv_ref[...] = (done < float(N_EXPERTS)).astype(I32)


def _plan(counts, top_idx, rank, n_tiles):
    t = top_idx.shape[0]
    tm = min(t, 1024)
    n_tiles_pad = -(-n_tiles // LANES) * LANES
    lane_rows = pl.BlockSpec((tm, LANES), lambda i: (i, 0))
    meta = pl.BlockSpec((1, n_tiles_pad), lambda i: (0, 0))
    return pl.pallas_call(
        functools.partial(_plan_kernel, n_tiles_pad=n_tiles_pad),
        out_shape=(jax.ShapeDtypeStruct((t, LANES), I32),
                   jax.ShapeDtypeStruct((1, n_tiles_pad), I32),
                   jax.ShapeDtypeStruct((1, n_tiles_pad), I32)),
        grid=(t // tm,),
        in_specs=[pl.BlockSpec((1, LANES), lambda i: (0, 0)), lane_rows, lane_rows],
        out_specs=(lane_rows, meta, meta),
        compiler_params=_cparams(("arbitrary",), 32),
        name="moe_plan",
    )(counts, top_idx, rank)


DISPATCH_TM = 256
DMA_UNROLL = 8


def _dispatch_kernel(pos_hbm, h_ref, xs_in, xs_out, pos_smem, sem_idx, sem_rows):
    del xs_in
    i = pl.program_id(0)
    n = DISPATCH_TM * TOP_K
    idx_copy = pltpu.make_async_copy(pos_hbm.at[pl.ds(pl.multiple_of(i * n, n), n)], pos_smem, sem_idx)
    idx_copy.start()
    idx_copy.wait()

    def row_copy(t, slot):
        src = h_ref.at[pl.ds(pl.multiple_of(t * ROW_TILE, ROW_TILE), ROW_TILE), :]
        dst = xs_out.at[pl.ds(pl.multiple_of(slot * ROW_TILE, ROW_TILE), ROW_TILE), :]
        return pltpu.make_async_copy(src, dst, sem_rows)

    def issue(t, carry):
        for k in range(TOP_K):
            row_copy(t, pos_smem[t * TOP_K + k]).start()
        return carry

    lax.fori_loop(0, DISPATCH_TM, issue, 0, unroll=DMA_UNROLL)

    def drain(t, carry):
        for k in range(TOP_K):
            row_copy(t, pos_smem[t * TOP_K + k]).wait()
        return carry

    lax.fori_loop(0, DISPATCH_TM, drain, 0, unroll=DMA_UNROLL)


def _dispatch(pos_flat, h_rows, n_rows):
    t = h_rows.shape[0] // ROW_TILE
    xs0 = jnp.zeros((n_rows * ROW_TILE, LANES), U32)
    return pl.pallas_call(
        _dispatch_kernel,
        out_shape=jax.ShapeDtypeStruct(xs0.shape, U32),
        grid=(t // DISPATCH_TM,),
        in_specs=[pl.BlockSpec(memory_space=pl.ANY),
                  pl.BlockSpec((DISPATCH_TM * ROW_TILE, LANES), lambda i: (i, 0)),
                  pl.BlockSpec(memory_space=pl.ANY)],
        out_specs=pl.BlockSpec(memory_space=pl.ANY),
        scratch_shapes=[pltpu.SMEM((DISPATCH_TM * TOP_K,), I32),
                        pltpu.SemaphoreType.DMA, pltpu.SemaphoreType.DMA],
        input_output_aliases={2: 0},
        compiler_params=_cparams(("arbitrary",), 32),
        name="moe_dispatch",
    )(pos_flat, h_rows, xs0)


CAST_ROWS = 256


def _cast_weights(w_ref, w_scr):
    def chunk(r, carry):
        rows = pl.ds(pl.multiple_of(r * CAST_ROWS, CAST_ROWS), CAST_ROWS)
        w_scr[rows, :] = w_ref[0, rows, :].astype(BF16)
        return carry

    lax.fori_loop(0, w_scr.shape[0] // CAST_ROWS, chunk, 0)


def _tile_flags(te_ref, tv_ref):
    j = pl.program_id(0)
    valid = tv_ref[j] != 0
    first = (j == 0) | (te_ref[j] != te_ref[jnp.maximum(j - 1, 0)])
    return valid, first


def _expert_up_kernel(te_ref, tv_ref, xs_ref, wgu_ref, bgu_ref, act_ref, w_scr):
    valid, first = _tile_flags(te_ref, tv_ref)
    f = act_ref.shape[1]

    @pl.when(valid & first)
    def _():
        _cast_weights(wgu_ref, w_scr)

    @pl.when(valid)
    def _():
        x = _load_packed_rows(xs_ref, 0, EXPERT_TM).astype(BF16)
        gu = _dot(x, w_scr[...]) + bgu_ref[0]
        gate = jnp.minimum(gu[:, :f], SWIGLU_LIMIT)
        up = jnp.clip(gu[:, f:], -SWIGLU_LIMIT, SWIGLU_LIMIT)
        act_ref[...] = ((up + 1.0) * (gate * _sigmoid(SWIGLU_ALPHA * gate))).astype(BF16)

    @pl.when(jnp.logical_not(valid))
    def _():
        act_ref[...] = jnp.zeros_like(act_ref)


def _expert_down_kernel(te_ref, tv_ref, act_ref, wd_ref, bd_ref, o_ref, w_scr):
    valid, first = _tile_flags(te_ref, tv_ref)

    @pl.when(valid & first)
    def _():
        _cast_weights(wd_ref, w_scr)

    @pl.when(valid)
    def _():
        _store_packed_rows(o_ref, _dot(act_ref[...], w_scr[...]) + bd_ref[0])

    @pl.when(jnp.logical_not(valid))
    def _():
        o_ref[...] = jnp.zeros_like(o_ref)


def _experts(tile_expert, tile_valid, xs, wgu, bgu, wd, bd, layer, n_tiles):
    n_layer, n_e, d, f2 = wgu.shape
    f = f2 // 2
    rows = pl.BlockSpec((EXPERT_TM * ROW_TILE, LANES), lambda j, te, tv: (j, 0))
    acts = pl.BlockSpec((EXPERT_TM, f), lambda j, te, tv: (j, 0))
    per_expert = lambda shape: pl.BlockSpec((None, 1) + shape, lambda j, te, tv: (layer, te[j], 0, 0))
    act = pl.pallas_call(
        _expert_up_kernel,
        out_shape=jax.ShapeDtypeStruct((n_tiles * EXPERT_TM, f), BF16),
        grid_spec=pltpu.PrefetchScalarGridSpec(
            num_scalar_prefetch=2,
            grid=(n_tiles,),
            in_specs=[rows, per_expert((d, f2)), per_expert((1, f2))],
            out_specs=acts,
            scratch_shapes=[pltpu.VMEM((d, f2), BF16)]),
        compiler_params=_cparams(("arbitrary",), 56),
        name="moe_expert_up",
    )(tile_expert, tile_valid, xs, wgu, bgu.reshape(n_layer, n_e, 1, f2))
    return pl.pallas_call(
        _expert_down_kernel,
        out_shape=jax.ShapeDtypeStruct(xs.shape, U32),
        grid_spec=pltpu.PrefetchScalarGridSpec(
            num_scalar_prefetch=2,
            grid=(n_tiles,),
            in_specs=[acts, per_expert((f, d)), per_expert((1, d))],
            out_specs=rows,
            scratch_shapes=[pltpu.VMEM((f, d), BF16)]),
        compiler_params=_cparams(("arbitrary",), 48),
        name="moe_expert_down",
    )(tile_expert, tile_valid, act, wd, bd.reshape(n_layer, n_e, 1, d))


COMBINE_TM = 128


def _combine_kernel(pos_hbm, ys_hbm, w_ref, x_ref, g_ref, o_ref, pos_smem, buf, sem_idx, sem_rows):
    i = pl.program_id(0)
    tm = COMBINE_TM
    n = 2 * tm * TOP_K
    half = i % 2
    idx_copy = pltpu.make_async_copy(pos_hbm.at[pl.ds(pl.multiple_of((i // 2) * n, n), n)], pos_smem, sem_idx)
    idx_copy.start()
    idx_copy.wait()

    def row_copy(t, k):
        slot = pos_smem[(half * tm + t) * TOP_K + k]
        src = ys_hbm.at[pl.ds(pl.multiple_of(slot * ROW_TILE, ROW_TILE), ROW_TILE), :]
        dst = buf.at[pl.ds(pl.multiple_of((k * tm + t) * ROW_TILE, ROW_TILE), ROW_TILE), :]
        return pltpu.make_async_copy(src, dst, sem_rows)

    def issue(t, carry):
        for k in range(TOP_K):
            row_copy(t, k).start()
        return carry

    lax.fori_loop(0, tm, issue, 0, unroll=DMA_UNROLL)

    def drain(t, carry):
        for k in range(TOP_K):
            row_copy(t, k).wait()
        return carry

    lax.fori_loop(0, tm, drain, 0, unroll=DMA_UNROLL)

    w = w_ref[...]
    y = jnp.zeros(x_ref.shape, F32)
    for k in range(TOP_K):
        y = y + w[:, k:k + 1] * _load_packed_rows(buf, k * tm, tm)
    o_ref[...] = x_ref[...] + g_ref[0] * y


def _combine(pos_flat, ys, top_w, x2, gate, seq):
    t, d = x2.shape
    tm = COMBINE_TM
    per_b = seq // tm
    return pl.pallas_call(
        _combine_kernel,
        out_shape=jax.ShapeDtypeStruct((t, d), F32),
        grid=(t // tm,),
        in_specs=[pl.BlockSpec(memory_space=pl.ANY),
                  pl.BlockSpec(memory_space=pl.ANY),
                  pl.BlockSpec((tm, LANES), lambda i: (i, 0)),
                  pl.BlockSpec((tm, d), lambda i: (i, 0)),
                  pl.BlockSpec((1, 1, d), lambda i: (i // per_b, 0, 0))],
        out_specs=pl.BlockSpec((tm, d), lambda i: (i, 0)),
        scratch_shapes=[pltpu.SMEM((2 * tm * TOP_K,), I32),
                        pltpu.VMEM((TOP_K * tm * ROW_TILE, LANES), U32),
                        pltpu.SemaphoreType.DMA, pltpu.SemaphoreType.DMA],
        compiler_params=_cparams(("arbitrary",), 32),
        name="moe_combine",
    )(pos_flat, ys, top_w, x2, gate)


def _final_norm_kernel(x_ref, g_ref, o_ref):
    x = x_ref[...]
    o_ref[...] = x * lax.rsqrt(jnp.mean(x * x, axis=-1, keepdims=True) + RMS_EPS) * g_ref[...]


def _final_norm(x2, g):
    t, d = x2.shape
    tm = min(t, 512)
    return pl.pallas_call(
        _final_norm_kernel,
        out_shape=jax.ShapeDtypeStruct((t, d), F32),
        grid=(t // tm,),
        in_specs=[pl.BlockSpec((tm, d), lambda i: (i, 0)), pl.BlockSpec((1, d), lambda i: (0, 0))],
        out_specs=pl.BlockSpec((tm, d), lambda i: (i, 0)),
        compiler_params=_cparams(("parallel",), 32),
        name="final_norm",
    )(x2, g.reshape(1, d))


def _spread_rope_cols(w):
    half = MLA_ROPE_DIM // 2
    z = jnp.zeros(w.shape[:-1] + (half,), w.dtype)
    return jnp.concatenate([w[..., :half], z, w[..., half:], z], axis=-1)


W_IN_WINDOW = PROJ_TN // LANES + 1


def _w_in_layout_kernel(b0_ref, sh_ref, *refs, n_cols):
    src = refs[:W_IN_WINDOW]
    aux_ref, o_ref = refs[W_IN_WINDOW:]
    n = pl.program_id(1)

    @pl.when(n != T_AUX)
    def _():
        x = jnp.concatenate([s[0] for s in src], axis=1)
        col = lax.broadcasted_iota(I32, x.shape, 1) + b0_ref[n] * LANES
        xb = jnp.where(col < n_cols, x, 0.0).astype(BF16)
        r = lax.broadcasted_iota(I32, (W_IN_WINDOW * LANES, PROJ_TN), 0)
        c = lax.broadcasted_iota(I32, (W_IN_WINDOW * LANES, PROJ_TN), 1)
        shift = (r == c + sh_ref[n]).astype(BF16)
        o_ref[0] = _dot(xb, shift).astype(BF16)

    @pl.when(n == T_AUX)
    def _():
        o_ref[0] = aux_ref[0]


def _layout_w_in(w_in):
    n_layer, d, n_cols = w_in.shape
    sizes = (SB_W, SB_W, SB_W, FOX_W, FOX_W, FOX_W, FOX_HEADS, MLA_Q_RANK, MLA_KV_RANK, MLA_ROPE_DIM,
             N_BRANCH * d)
    offs = [int(v) for v in np.concatenate([[0], np.cumsum(sizes)])]
    n_gate = N_BRANCH * d // PROJ_TN
    starts = [offs[0], offs[1], offs[2], offs[3], offs[4], offs[5], offs[7], 0]
    starts += [offs[10] + PROJ_TN * g for g in range(n_gate)]
    b0 = jnp.asarray([s // LANES for s in starts], I32)
    sh = jnp.asarray([s % LANES for s in starts], I32)

    ff = jnp.zeros((n_layer, d, LANES), w_in.dtype).at[:, :, :FOX_HEADS].set(w_in[:, :, offs[6]:offs[7]])
    aux = jnp.concatenate([w_in[:, :, offs[8]:offs[9]], _spread_rope_cols(w_in[:, :, offs[9]:offs[10]]), ff],
                          axis=2).astype(BF16)

    def src_spec(k):
        return pl.BlockSpec((1, d, LANES), lambda l, n, b0, sh: (l, 0, b0[n] + k))

    n_tiles = len(starts)
    return pl.pallas_call(
        functools.partial(_w_in_layout_kernel, n_cols=n_cols),
        out_shape=jax.ShapeDtypeStruct((n_layer, d, n_tiles * PROJ_TN), BF16),
        grid_spec=pltpu.PrefetchScalarGridSpec(
            num_scalar_prefetch=2,
            grid=(n_layer, n_tiles),
            in_specs=[src_spec(k) for k in range(W_IN_WINDOW)]
            + [pl.BlockSpec((1, d, PROJ_TN), lambda l, n, b0, sh: (l, 0, 0))],
            out_specs=pl.BlockSpec((1, d, PROJ_TN), lambda l, n, b0, sh: (l, 0, n))),
        compiler_params=_cparams(("parallel", "arbitrary"), 48),
        name="w_in_layout",
    )(b0, sh, *([w_in] * W_IN_WINDOW), aux)


def _layout_w_q_up(w):
    r = w.shape[0]
    w3 = w.reshape(r, MLA_HEADS, MLA_QK_DIM)
    nope = w3[:, :, :MLA_NOPE_DIM].reshape(r, MLA_HEADS * MLA_NOPE_DIM)
    rope = _spread_rope_cols(w3[:, :, MLA_NOPE_DIM:]).reshape(r, MLA_HEADS * LANES)
    return jnp.concatenate([nope, rope], axis=1).astype(BF16)


def _layout_w_kv_up(w):
    r = w.shape[0]
    w3 = w.reshape(r, MLA_HEADS, MLA_NOPE_DIM + MLA_V_DIM)
    kn = w3[:, :, :MLA_NOPE_DIM].reshape(r, MLA_HEADS * MLA_NOPE_DIM)
    v = w3[:, :, MLA_NOPE_DIM:].reshape(r, MLA_HEADS * MLA_V_DIM)
    return jnp.concatenate([kn, v], axis=1).astype(BF16)


def kernel(x, c, positions, w_ada, b_ada, g_mix_norm, g_ffn_norm, w_in, b_forget, g_q_norm, g_kv_norm,
           w_q_up, w_kv_up, w_sb_proj, w_fox_proj, w_mla_proj, w_out, w_router, b_router, w_gate_up,
           b_gate_up, w_down, b_down, g_final):
    batch, seq, d = x.shape
    t = batch * seq
    depth = w_ada.shape[0]
    n_rows = t * TOP_K + N_EXPERTS * EXPERT_TM
    n_tiles = n_rows // EXPERT_TM

    mod = _adaln(c, w_ada, b_ada)
    cos_t, sin_t = _rope_tables(positions)
    w1_all = _layout_w_in(w_in)
    x2 = x.reshape(t, d)

    for layer in range(depth):
        mvec = lambda k: mod[layer, :, k].reshape(batch, 1, d)
        proj, aux = _premix(x2, g_mix_norm[layer], mvec(1), mvec(0), w1_all, layer, seq)
        qn, qr, kn, v, kr = _mla_prep(proj, cos_t, sin_t, g_q_norm[layer], g_kv_norm[layer],
                                      _layout_w_q_up(w_q_up[layer]), _layout_w_kv_up(w_kv_up[layer]))
        frep, ft = _fox_gates(aux, b_forget[layer], batch, seq)
        y_sb = _sb_attention(proj, batch, seq)
        y_fox = _fox_attention(proj, frep, ft, batch, seq)
        y_mla = _mla_attention(qn, qr, kn, kr, v, batch, seq)
        merged = _merge(y_sb, y_fox, y_mla, w_sb_proj[layer].astype(BF16), w_fox_proj[layer].astype(BF16),
                        w_mla_proj[layer].astype(BF16), proj, d)
        x2 = _outproj(merged, w_out[layer].astype(BF16), x2, mvec(2), seq)

        h_rows, top_idx, top_w, rank, counts = _router(x2, g_ffn_norm[layer], mvec(4), mvec(3),
                                                       w_router[layer], b_router[layer], seq)
        pos, tile_expert, tile_valid = _plan(counts, top_idx, rank, n_tiles)
        pos_flat = pos[:, :TOP_K].reshape(t * TOP_K)
        xs = _dispatch(pos_flat, h_rows, n_rows)
        ys = _experts(tile_expert[0], tile_valid[0], xs, w_gate_up, b_gate_up, w_down, b_down, layer, n_tiles)
        x2 = _combine(pos_flat, ys, top_w, x2, mvec(5), seq)

    return _final_norm(x2, g_final).reshape(batch, seq, d)
```
